```python
import math
import jax
import jax.numpy as jnp
from jax import lax
import numpy as np

D_MODEL = 1024
BATCH = 16
SEQ = 4096
DEPTH = 4
DEC_BATCH = 8
DEC_SEQ = 64
PAST_LEN = 2048

CHUNK = 64
Q_BLOCK = 128
N_MIXERS = 2
N_MLA = (DEPTH + 1) // 2
N_GLA = DEPTH // 2

MLA_HEADS = 16
MLA_NOPE = 64
MLA_ROPE = 32
MLA_V = 64
MLA_Q_LORA = 384
MLA_KV_LORA = 256
MLA_SCALE = (MLA_NOPE + MLA_ROPE) ** -0.5
ROPE_THETA = 10000.0

GLA_HEADS = 4
GLA_DK = D_MODEL // 2 // GLA_HEADS
GLA_DV = D_MODEL // GLA_HEADS
GLA_GATE_RANK = 16
GLA_TAU = 16.0

N_EXPERTS = 64
TOP_K = 8
N_EXPERT_GROUPS = 8
TOPK_GROUPS = 4
EXPERT_FF = 256
SHARED_FF = 256
ROUTED_SCALE = 2.5

DEEPNORM_ALPHA = (2.0 * DEPTH) ** 0.25
DEEPNORM_BETA = (8.0 * DEPTH) ** -0.25
NORM_EPS = 1e-5

kernel_name = "mla_gla_moe_deepnorm_stream_step"


def layer_norm(x, g, b):
    xf = x.astype(jnp.float32)
    mu = jnp.mean(xf, -1, keepdims=True)
    var = jnp.mean(jnp.square(xf - mu), -1, keepdims=True)
    y = (xf - mu) * lax.rsqrt(var + NORM_EPS) * g.astype(jnp.float32) + b.astype(jnp.float32)
    return y.astype(x.dtype)


def rms_norm(x, g):
    xf = x.astype(jnp.float32)
    y = xf * lax.rsqrt(jnp.mean(jnp.square(xf), -1, keepdims=True) + NORM_EPS)
    return (y * g.astype(jnp.float32)).astype(x.dtype)


def rope(x, pos):
    half = x.shape[-1] // 2
    inv = ROPE_THETA ** (-jnp.arange(half, dtype=jnp.float32) / half)
    ang = pos.astype(jnp.float32)[:, None] * inv[None, :]
    shape = (ang.shape[0],) + (1,) * (x.ndim - 3) + (half,)
    cos = jnp.cos(ang).reshape(shape)
    sin = jnp.sin(ang).reshape(shape)
    xf = x.astype(jnp.float32)
    x1, x2 = xf[..., :half], xf[..., half:]
    return jnp.concatenate([x1 * cos - x2 * sin, x1 * sin + x2 * cos], -1).astype(x.dtype)


def mla_project(x, pos, w_dq, q_norm, w_uq, w_dkv, kv_norm):
    B, L, _ = x.shape
    c_q = rms_norm(x @ w_dq, q_norm)
    q = (c_q @ w_uq).reshape(B, L, MLA_HEADS, MLA_NOPE + MLA_ROPE)
    q_nope, q_pe = q[..., :MLA_NOPE], rope(q[..., MLA_NOPE:], pos)
    kv = x @ w_dkv
    c_kv = rms_norm(kv[..., :MLA_KV_LORA], kv_norm)
    k_pe = rope(kv[..., MLA_KV_LORA:], pos)
    return q_nope, q_pe, c_kv, k_pe


def mla_expand(c_kv, w_uk, w_uv):
    B, L, _ = c_kv.shape
    k_nope = (c_kv @ w_uk).reshape(B, L, MLA_HEADS, MLA_NOPE)
    v = (c_kv @ w_uv).reshape(B, L, MLA_HEADS, MLA_V)
    return k_nope, v


def mla_attend(q_nope, q_pe, k_nope, k_pe, v, mask):
    s = (jnp.einsum('bqhn,bkhn->bhqk', q_nope, k_nope, preferred_element_type=jnp.float32)
         + jnp.einsum('bqhr,bkr->bhqk', q_pe, k_pe, preferred_element_type=jnp.float32)) * MLA_SCALE
    if mask is not None:
        s = jnp.where(mask, s, -jnp.inf)
    p = jax.nn.softmax(s, axis=-1).astype(v.dtype)
    return jnp.einsum('bhqk,bkhv->bqhv', p, v)


def mla_prompt(x, w_dq, q_norm, w_uq, w_dkv, kv_norm, w_uk, w_uv, w_o):
    B, L, _ = x.shape
    pos = jnp.arange(L)
    q_nope, q_pe, c_kv, k_pe = mla_project(x, pos, w_dq, q_norm, w_uq, w_dkv, kv_norm)
    k_nope, v = mla_expand(c_kv, w_uk, w_uv)
    nb = L // Q_BLOCK
    k_chunk = jnp.arange(L) // CHUNK

    def block(args):
        qn, qp, i = args
        q_chunk = (i * Q_BLOCK + jnp.arange(Q_BLOCK)) // CHUNK
        mask = k_chunk[None, :] <= q_chunk[:, None]
        return mla_attend(qn, qp, k_nope, k_pe, v, mask)

    def to_blocks(t):
        return jnp.moveaxis(t.reshape((B, nb, Q_BLOCK) + t.shape[2:]), 1, 0)

    o = lax.map(block, (to_blocks(q_nope), to_blocks(q_pe), jnp.arange(nb)))
    o = jnp.moveaxis(o, 0, 1).reshape(B, L, MLA_HEADS * MLA_V)
    return o @ w_o, c_kv, k_pe


def mla_sample(x, c_past, kpe_past, w_dq, q_norm, w_uq, w_dkv, kv_norm, w_uk, w_uv, w_o):
    B, L, _ = x.shape
    pos = c_past.shape[1] + jnp.arange(L)
    q_nope, q_pe, c_kv, k_pe = mla_project(x, pos, w_dq, q_norm, w_uq, w_dkv, kv_norm)
    c_all = jnp.concatenate([c_past, c_kv], axis=1)
    kpe_all = jnp.concatenate([kpe_past, k_pe], axis=1)
    k_nope, v = mla_expand(c_all, w_uk, w_uv)
    o = mla_attend(q_nope, q_pe, k_nope, kpe_all, v, None)
    return o.reshape(B, L, MLA_HEADS * MLA_V) @ w_o, c_kv, k_pe


def gla_mixer(x, s0, w_q, w_k, w_v, w_a1, w_a2, b_a, w_g, g_norm, w_o):
    B, L, _ = x.shape
    C = CHUNK if L % CHUNK == 0 else L
    n = L // C
    f32 = jnp.float32
    q = (x @ w_q).reshape(B, L, GLA_HEADS, GLA_DK).astype(f32) * (GLA_DK ** -0.5)
    k = (x @ w_k).reshape(B, L, GLA_HEADS, GLA_DK).astype(f32)
    v = (x @ w_v).reshape(B, L, GLA_HEADS, GLA_DV).astype(f32)
    log_a = (jax.nn.log_sigmoid(((x @ w_a1) @ w_a2 + b_a).astype(f32)) / GLA_TAU).reshape(B, L, GLA_HEADS, GLA_DK)

    def to_chunks(t):
        return t.reshape(B, n, C, GLA_HEADS, t.shape[-1]).transpose(1, 0, 3, 2, 4)

    causal = jnp.arange(C)[:, None] >= jnp.arange(C)[None, :]

    def step(S, inp):
        qc, kc, vc, lac = inp
        b = jnp.cumsum(lac, axis=2)
        decay = jnp.exp(jnp.where(causal[:, :, None], b[:, :, :, None, :] - b[:, :, None, :, :], -jnp.inf))
        att = jnp.einsum('bhtd,bhsd,bhtsd->bhts', qc, kc, decay)
        o = (jnp.einsum('bhts,bhsv->bhtv', att, vc)
             + jnp.einsum('bhtd,bhdv->bhtv', qc * jnp.exp(b), S))
        b_end = b[:, :, -1, :]
        S = (jnp.exp(b_end)[..., None] * S
             + jnp.einsum('bhsd,bhsv->bhdv', kc * jnp.exp(b_end[:, :, None, :] - b), vc))
        return S, o

    S, o = lax.scan(step, s0.astype(f32), (to_chunks(q), to_chunks(k), to_chunks(v), to_chunks(log_a)))
    o = o.transpose(1, 0, 3, 2, 4).reshape(B, L, GLA_HEADS, GLA_DV)
    o = rms_norm(o, g_norm).astype(x.dtype)
    o = o * jax.nn.silu(x @ w_g).reshape(B, L, GLA_HEADS, GLA_DV)
    return o.reshape(B, L, GLA_HEADS * GLA_DV) @ w_o, S.astype(x.dtype)


def moe(x, w_router, router_bias, w_gate, w_up, w_down, ws_gate, ws_up, ws_down):
    f32 = jnp.float32
    shp = x.shape
    t = x.reshape(-1, shp[-1])
    T = t.shape[0]
    scores = jax.nn.sigmoid((t @ w_router).astype(f32))
    sel = scores + router_bias.astype(f32)
    grp = sel.reshape(T, N_EXPERT_GROUPS, N_EXPERTS // N_EXPERT_GROUPS)
    grp_score = jnp.sum(lax.top_k(grp, 2)[0], axis=-1)
    _, gidx = lax.top_k(grp_score, TOPK_GROUPS)
    gmask = jnp.sum(jax.nn.one_hot(gidx, N_EXPERT_GROUPS, dtype=f32), axis=1)
    emask = jnp.repeat(gmask, N_EXPERTS // N_EXPERT_GROUPS, axis=-1) > 0
    _, idx = lax.top_k(jnp.where(emask, sel, -jnp.inf), TOP_K)
    w = jnp.take_along_axis(scores, idx, axis=-1)
    w = w / jnp.sum(w, axis=-1, keepdims=True) * ROUTED_SCALE
    gates = jnp.einsum('tk,tke->et', w, jax.nn.one_hot(idx, N_EXPERTS, dtype=f32)).astype(x.dtype)

    def expert(acc, p):
        wg, wu, wd, g = p
        h = jax.nn.silu(t @ wg) * (t @ wu)
        return acc + (h @ wd) * g[:, None], None

    routed, _ = lax.scan(expert, jnp.zeros_like(t), (w_gate, w_up, w_down, gates))
    shared = (jax.nn.silu(t @ ws_gate) * (t @ ws_up)) @ ws_down
    return (routed + shared).reshape(shp)


def setup_inputs(seed: int = 0) -> dict:
    key = jax.random.key(seed)
    ks = iter(jax.random.split(key, 40))
    f32 = jnp.float32

    def dense(shape, fan_in, gain=1.0):
        return jax.random.normal(next(ks), shape, f32) * (gain * fan_in ** -0.5)

    def gain_vec(shape):
        return 1.0 + 0.02 * jax.random.normal(next(ks), shape, f32)

    def small(shape, scale):
        return scale * jax.random.normal(next(ks), shape, f32)

    beta = DEEPNORM_BETA
    return {
        "x_prompt": jax.random.normal(next(ks), (BATCH, SEQ, D_MODEL), f32),
        "x_sample": jax.random.normal(next(ks), (DEC_BATCH, DEC_SEQ, D_MODEL), f32),
        "cache_ckv": jax.random.normal(next(ks), (N_MLA, DEC_BATCH, PAST_LEN, MLA_KV_LORA), f32),
        "cache_kpe": jax.random.normal(next(ks), (N_MLA, DEC_BATCH, PAST_LEN, MLA_ROPE), f32),
        "state_gla": small((N_GLA, DEC_BATCH, GLA_HEADS, GLA_DK, GLA_DV), 0.5),
        "mla_w_dq": dense((N_MLA, D_MODEL, MLA_Q_LORA), D_MODEL),
        "mla_q_norm": gain_vec((N_MLA, MLA_Q_LORA)),
        "mla_w_uq": dense((N_MLA, MLA_Q_LORA, MLA_HEADS * (MLA_NOPE + MLA_ROPE)), MLA_Q_LORA),
        "mla_w_dkv": dense((N_MLA, D_MODEL, MLA_KV_LORA + MLA_ROPE), D_MODEL),
        "mla_kv_norm": gain_vec((N_MLA, MLA_KV_LORA)),
        "mla_w_uk": dense((N_MLA, MLA_KV_LORA, MLA_HEADS * MLA_NOPE), MLA_KV_LORA),
        "mla_w_uv": dense((N_MLA, MLA_KV_LORA, MLA_HEADS * MLA_V), MLA_KV_LORA, beta),
        "mla_w_o": dense((N_MLA, MLA_HEADS * MLA_V, D_MODEL), MLA_HEADS * MLA_V, beta),
        "gla_w_q": dense((N_GLA, D_MODEL, GLA_HEADS * GLA_DK), D_MODEL),
        "gla_w_k": dense((N_GLA, D_MODEL, GLA_HEADS * GLA_DK), D_MODEL),
        "gla_w_v": dense((N_GLA, D_MODEL, GLA_HEADS * GLA_DV), D_MODEL, beta),
        "gla_w_a1": dense((N_GLA, D_MODEL, GLA_GATE_RANK), D_MODEL),
        "gla_w_a2": dense((N_GLA, GLA_GATE_RANK, GLA_HEADS * GLA_DK), GLA_GATE_RANK),
        "gla_b_a": small((N_GLA, GLA_HEADS * GLA_DK), 0.1),
        "gla_w_g": dense((N_GLA, D_MODEL, GLA_HEADS * GLA_DV), D_MODEL),
        "gla_g_norm": gain_vec((N_GLA, GLA_DV)),
        "gla_w_o": dense((N_GLA, GLA_HEADS * GLA_DV, D_MODEL), GLA_HEADS * GLA_DV, beta),
        "ln1_g": gain_vec((DEPTH, D_MODEL)),
        "ln1_b": small((DEPTH, D_MODEL), 0.02),
        "ln2_g": gain_vec((DEPTH, D_MODEL)),
        "ln2_b": small((DEPTH, D_MODEL), 0.02),
        "moe_w_router": dense((DEPTH, D_MODEL, N_EXPERTS), D_MODEL),
        "moe_router_bias": small((DEPTH, N_EXPERTS), 0.01),
        "moe_w_gate": dense((DEPTH, N_EXPERTS, D_MODEL, EXPERT_FF), D_MODEL),
        "moe_w_up": dense((DEPTH, N_EXPERTS, D_MODEL, EXPERT_FF), D_MODEL),
        "moe_w_down": dense((DEPTH, N_EXPERTS, EXPERT_FF, D_MODEL), EXPERT_FF, beta),
        "moe_ws_gate": dense((DEPTH, D_MODEL, SHARED_FF), D_MODEL),
        "moe_ws_up": dense((DEPTH, D_MODEL, SHARED_FF), D_MODEL),
        "moe_ws_down": dense((DEPTH, SHARED_FF, D_MODEL), SHARED_FF, beta),
    }


def reference(x_prompt, x_sample, cache_ckv, cache_kpe, state_gla,
              mla_w_dq, mla_q_norm, mla_w_uq, mla_w_dkv, mla_kv_norm, mla_w_uk, mla_w_uv, mla_w_o,
              gla_w_q, gla_w_k, gla_w_v, gla_w_a1, gla_w_a2, gla_b_a, gla_w_g, gla_g_norm, gla_w_o,
              ln1_g, ln1_b, ln2_g, ln2_b,
              moe_w_router, moe_router_bias, moe_w_gate, moe_w_up, moe_w_down,
              moe_ws_gate, moe_ws_up, moe_ws_down):
    alpha = DEEPNORM_ALPHA
    yp, ys = x_prompt, x_sample
    ckv_p, kpe_p, gla_p, ckv_s, kpe_s, gla_s = [], [], [], [], [], []
    for i in range(DEPTH):
        j = i // N_MIXERS
        if i % N_MIXERS == 0:
            mw = (mla_w_dq[j], mla_q_norm[j], mla_w_uq[j], mla_w_dkv[j], mla_kv_norm[j],
                  mla_w_uk[j], mla_w_uv[j], mla_w_o[j])
            hp, c_new, k_new = mla_prompt(yp, *mw)
            ckv_p.append(c_new)
            kpe_p.append(k_new)
            hs, c_new, k_new = mla_sample(ys, cache_ckv[j], cache_kpe[j], *mw)
            ckv_s.append(c_new)
            kpe_s.append(k_new)
        else:
            gw = (gla_w_q[j], gla_w_k[j], gla_w_v[j], gla_w_a1[j], gla_w_a2[j], gla_b_a[j],
                  gla_w_g[j], gla_g_norm[j], gla_w_o[j])
            s0 = jnp.zeros((yp.shape[0], GLA_HEADS, GLA_DK, GLA_DV), yp.dtype)
            hp, st = gla_mixer(yp, s0, *gw)
            gla_p.append(st)
            hs, st = gla_mixer(ys, state_gla[j], *gw)
            gla_s.append(st)
        yp = layer_norm(alpha * yp + hp, ln1_g[i], ln1_b[i])
        ys = layer_norm(alpha * ys + hs, ln1_g[i], ln1_b[i])
        ew = (moe_w_router[i], moe_router_bias[i], moe_w_gate[i], moe_w_up[i], moe_w_down[i],
              moe_ws_gate[i], moe_ws_up[i], moe_ws_down[i])
        yp = layer_norm(alpha * yp + moe(yp, *ew), ln2_g[i], ln2_b[i])
        ys = layer_norm(alpha * ys + moe(ys, *ew), ln2_g[i], ln2_b[i])
    return (yp, ys, jnp.stack(ckv_p), jnp.stack(kpe_p), jnp.stack(gla_p),
            jnp.stack(ckv_s), jnp.stack(kpe_s), jnp.stack(gla_s))
```

```python
import functools

import jax
import jax.numpy as jnp
from jax import lax
from jax.experimental import pallas as pl
from jax.experimental.pallas import tpu as pltpu
from jax.experimental.pallas import tpu_sc as plsc

F32, BF16, I32 = jnp.float32, jnp.bfloat16, jnp.int32

D_MODEL = 1024
DEPTH = 4
CHUNK = 64
MLA_HEADS = 16
MLA_NOPE = 64
MLA_ROPE = 32
MLA_V = 64
MLA_Q_LORA = 384
MLA_KV_LORA = 256
MLA_SCALE = (MLA_NOPE + MLA_ROPE) ** -0.5
ROPE_THETA = 10000.0
GLA_HEADS = 4
GLA_DK = 128
GLA_DV = 256
GLA_GATE_RANK = 16
GLA_TAU = 16.0
N_EXPERTS = 64
TOP_K = 8
N_GROUPS = 8
GROUP_SIZE = N_EXPERTS // N_GROUPS
TOPK_GROUPS = 4
EXPERT_FF = 256
ROUTED_SCALE = 2.5
DEEPNORM_ALPHA = (2.0 * DEPTH) ** 0.25
NORM_EPS = 1e-5

LANES = 128
HEAD_PAD = 128
SUB = 16
SC_WINDOW = 128
HALF = D_MODEL // 4
VMEM_LIMIT = 48 * 1024 * 1024


def _dot(a, b):
    return jnp.dot(a, b, preferred_element_type=F32)


def _dot_nt(a, b):
    return lax.dot_general(a, b, (((1,), (1,)), ((), ())), preferred_element_type=F32)


def _dot_tn(a, b):
    return lax.dot_general(a, b, (((0,), (0,)), ((), ())), preferred_element_type=F32)


def _params(*sem):
    return pltpu.CompilerParams(dimension_semantics=sem, vmem_limit_bytes=VMEM_LIMIT)


def _sigmoid(x):
    return 1.0 / (1.0 + jnp.exp(-x))


def _rms(x, g):
    return x * lax.rsqrt(jnp.mean(x * x, -1, keepdims=True) + NORM_EPS) * g


def _layer_norm(z, g, b):
    mu = jnp.mean(z, -1, keepdims=True)
    zc = z - mu
    var = jnp.mean(zc * zc, -1, keepdims=True)
    return zc * lax.rsqrt(var + NORM_EPS) * g + b


def _rope128(x, cos, msin, sin):
    return x * cos + pltpu.roll(x, LANES - 16, 1) * msin + pltpu.roll(x, 16, 1) * sin


def _pack_halves(y):
    bits = lax.bitcast_convert_type(y.astype(BF16).astype(F32), I32)
    half = D_MODEL // 2
    p = (bits[:, :half] & jnp.int32(-65536)) | lax.shift_right_logical(bits[:, half:], jnp.int32(16))
    return p[:, :HALF], p[:, HALF:]


def _unpack_halves(a, b):
    p = jnp.concatenate([a, b], 1)
    hi = lax.bitcast_convert_type(p & jnp.int32(-65536), F32)
    lo = lax.bitcast_convert_type(lax.shift_left(p, jnp.int32(16)), F32)
    return jnp.concatenate([hi, lo], 1)


def _mla_proj_kernel(x_ref, cos_ref, msin_ref, sin_ref, wdq_ref, qn_ref, wuq_ref, wkc_ref, kvn_ref,
                     wkp_ref, q_ref, ckv_ref, kpe_ref):
    xb = x_ref[...].astype(BF16)
    cos, msin, sin = cos_ref[...], msin_ref[...], sin_ref[...]
    cq = _rms(_dot(xb, wdq_ref[...]), qn_ref[...]).astype(BF16)
    for hp in range(MLA_HEADS // 2):
        q2 = _dot(cq, wuq_ref[:, hp * 2 * HEAD_PAD:(hp + 1) * 2 * HEAD_PAD])
        for s in range(2):
            c0 = (2 * hp + s) * HEAD_PAD
            q_ref[:, c0:c0 + HEAD_PAD] = _rope128(q2[:, s * HEAD_PAD:(s + 1) * HEAD_PAD], cos, msin, sin).astype(BF16)
    ckv_ref[...] = _rms(_dot(xb, wkc_ref[...]), kvn_ref[...])
    kp = _rope128(_dot(xb, wkp_ref[...]), cos, msin, sin)
    kpe_ref[...] = kp[:, :MLA_ROPE]


def _mla_proj(x, tabs, w, tm, n_prompt_tiles, tiles_per_seq):
    t = x.shape[0]
    cos, msin, sin = tabs
    wdq, qn, wuq, wkc, kvn, wkp = w
    row = lambda i: (i, 0)
    fix = lambda i: (0, 0)
    tab = lambda i: (jnp.where(i < n_prompt_tiles, i % tiles_per_seq, tiles_per_seq), 0)
    full = lambda a: pl.BlockSpec(a.shape, fix)
    return pl.pallas_call(
        _mla_proj_kernel,
        grid=(t // tm,),
        in_specs=[pl.BlockSpec((tm, D_MODEL), row),
                  pl.BlockSpec((tm, LANES), tab), pl.BlockSpec((tm, LANES), tab), pl.BlockSpec((tm, LANES), tab),
                  full(wdq), full(qn), full(wuq), full(wkc), full(kvn), full(wkp)],
        out_specs=[pl.BlockSpec((tm, MLA_HEADS * HEAD_PAD), row),
                   pl.BlockSpec((tm, MLA_KV_LORA), row),
                   pl.BlockSpec((tm, MLA_ROPE), row)],
        out_shape=[jax.ShapeDtypeStruct((t, MLA_HEADS * HEAD_PAD), BF16),
                   jax.ShapeDtypeStruct((t, MLA_KV_LORA), F32),
                   jax.ShapeDtypeStruct((t, MLA_ROPE), F32)],
        compiler_params=_params("parallel"),
        name="mla_proj",
    )(x, cos, msin, sin, wdq, qn, wuq, wkc, kvn, wkp)


def _kv_expand_kernel(ckv_ref, kpe_ref, wkc_ref, wke_ref, wuv_ref, k_ref, v_ref):
    c = ckv_ref[...].astype(BF16)
    p = kpe_ref[...].astype(BF16)
    k_ref[...] = (_dot(c, wkc_ref[...]) + _dot(p, wke_ref[...])).astype(BF16)
    v_ref[...] = _dot(c, wuv_ref[...]).astype(BF16)


def _kv_expand(ckv, kpe, w, tm, n_rows):
    wkc, wke, wuv = w
    row = lambda i: (i, 0)
    fix = lambda i: (0, 0)
    full = lambda a: pl.BlockSpec(a.shape, fix)
    return pl.pallas_call(
        _kv_expand_kernel,
        grid=(n_rows // tm,),
        in_specs=[pl.BlockSpec((tm, MLA_KV_LORA), row), pl.BlockSpec((tm, MLA_ROPE), row),
                  full(wkc), full(wke), full(wuv)],
        out_specs=[pl.BlockSpec((tm, MLA_HEADS * HEAD_PAD), row),
                   pl.BlockSpec((tm, MLA_HEADS * MLA_V), row)],
        out_shape=[jax.ShapeDtypeStruct((n_rows, MLA_HEADS * HEAD_PAD), BF16),
                   jax.ShapeDtypeStruct((n_rows, MLA_HEADS * MLA_V), BF16)],
        compiler_params=_params("parallel"),
        name="mla_kv_expand",
    )(ckv, kpe, wkc, wke, wuv)


def _attn_kernel(q_ref, k_ref, v_ref, o_ref, m_sc, l_sc, acc_sc, *, causal, nk):
    qi = pl.program_id(2)
    ki = pl.program_id(3)
    tq, tk = q_ref.shape[0], k_ref.shape[0]

    @pl.when(ki == 0)
    def _():
        m_sc[...] = jnp.full(m_sc.shape, -jnp.inf, F32)
        l_sc[...] = jnp.zeros(l_sc.shape, F32)
        acc_sc[...] = jnp.zeros(acc_sc.shape, F32)

    def step(masked):
        v = v_ref[...]
        if masked:
            vis = (lax.broadcasted_iota(I32, (tq, tk), 1) // CHUNK) <= (lax.broadcasted_iota(I32, (tq, tk), 0) // CHUNK)
        for h in range(2):
            s = _dot_nt(q_ref[:, h * HEAD_PAD:(h + 1) * HEAD_PAD], k_ref[:, h * HEAD_PAD:(h + 1) * HEAD_PAD])
            if masked:
                s = jnp.where(vis, s, -jnp.inf)
            m_prev = m_sc[h]
            m_new = jnp.maximum(m_prev, jnp.max(s, -1, keepdims=True))
            a = jnp.exp(m_prev - m_new)
            p = jnp.exp(s - m_new[:, :1])
            l_sc[h] = a * l_sc[h] + jnp.sum(p, -1, keepdims=True)
            acc_sc[h] = a * acc_sc[h] + _dot(p.astype(BF16), v)
            m_sc[h] = m_new

    if causal:
        pl.when(ki < qi)(lambda: step(False))
        pl.when(ki == qi)(lambda: step(True))
        last = qi
    else:
        step(False)
        last = nk - 1

    @pl.when(ki == last)
    def _():
        lane = lax.broadcasted_iota(I32, (tq, 2 * MLA_V), 1)
        o = jnp.where(lane < MLA_V, acc_sc[0] / l_sc[0], acc_sc[1] / l_sc[1])
        o_ref[...] = o.astype(BF16)


def _attention(q, k, v, *, nb, nq, nk, tq, tk, q_blk0, causal):
    hp = MLA_HEADS // 2
    kmap = (lambda b, h, qi, ki: (b * nk + jnp.minimum(ki, qi), h)) if causal else (lambda b, h, qi, ki: (b * nk + ki, h))
    return pl.pallas_call(
        functools.partial(_attn_kernel, causal=causal, nk=nk),
        grid=(nb, hp, nq, nk),
        in_specs=[pl.BlockSpec((tq, 2 * HEAD_PAD), lambda b, h, qi, ki: (q_blk0 + b * nq + qi, h)),
                  pl.BlockSpec((tk, 2 * HEAD_PAD), kmap),
                  pl.BlockSpec((tk, 2 * MLA_V), kmap)],
        out_specs=pl.BlockSpec((tq, 2 * MLA_V), lambda b, h, qi, ki: (b * nq + qi, h)),
        out_shape=jax.ShapeDtypeStruct((nb * nq * tq, MLA_HEADS * MLA_V), BF16),
        scratch_shapes=[pltpu.VMEM((2, tq, LANES), F32), pltpu.VMEM((2, tq, LANES), F32),
                        pltpu.VMEM((2, tq, 2 * MLA_V), F32)],
        compiler_params=_params("parallel", "parallel", "parallel", "arbitrary"),
        name="mla_attention_causal" if causal else "mla_attention_full",
    )(q, k, v)


def _proj_ln_kernel(o_ref, x_ref, w_ref, g_ref, b_ref, y_ref, ya_ref, yb_ref):
    z = DEEPNORM_ALPHA * x_ref[...] + _dot(o_ref[...], w_ref[...])
    y = _layer_norm(z, g_ref[...], b_ref[...])
    y_ref[...] = y
    a, b = _pack_halves(y)
    ya_ref[...] = a
    yb_ref[...] = b


def _proj_ln(o, x, w, g, b, tm):
    t = x.shape[0]
    row = lambda i: (i, 0)
    fix = lambda i: (0, 0)
    return pl.pallas_call(
        _proj_ln_kernel,
        grid=(t // tm,),
        in_specs=[pl.BlockSpec((tm, D_MODEL), row), pl.BlockSpec((tm, D_MODEL), row),
                  pl.BlockSpec(w.shape, fix), pl.BlockSpec(g.shape, fix), pl.BlockSpec(b.shape, fix)],
        out_specs=[pl.BlockSpec((tm, D_MODEL), row), pl.BlockSpec((tm, HALF), row), pl.BlockSpec((tm, HALF), row)],
        out_shape=[jax.ShapeDtypeStruct((t, D_MODEL), F32), jax.ShapeDtypeStruct((t, HALF), I32),
                   jax.ShapeDtypeStruct((t, HALF), I32)],
        compiler_params=_params("parallel"),
        name="mixer_out_ln1",
    )(o, x, w, g, b)


def _gla_proj_kernel(x_ref, wq_ref, wk_ref, wv_ref, wg_ref, wa1_ref, wa2_ref, ba_ref,
                     q_ref, k_ref, v_ref, g_ref, la_ref):
    xb = x_ref[...].astype(BF16)
    q_ref[...] = (_dot(xb, wq_ref[...]) * (GLA_DK ** -0.5)).astype(BF16)
    k_ref[...] = _dot(xb, wk_ref[...]).astype(BF16)
    v_ref[...] = _dot(xb, wv_ref[...]).astype(BF16)
    gz = _dot(xb, wg_ref[...])
    g_ref[...] = (gz * _sigmoid(gz)).astype(BF16)
    z = _dot(_dot(xb, wa1_ref[...]).astype(BF16), wa2_ref[...]) + ba_ref[...]
    la_ref[...] = (jnp.minimum(z, 0.0) - jnp.log(1.0 + jnp.exp(-jnp.abs(z)))) * (1.0 / GLA_TAU)


def _gla_proj(x, w, tm):
    t = x.shape[0]
    row = lambda i: (i, 0)
    fix = lambda i: (0, 0)
    dk, dv = GLA_HEADS * GLA_DK, GLA_HEADS * GLA_DV
    return pl.pallas_call(
        _gla_proj_kernel,
        grid=(t // tm,),
        in_specs=[pl.BlockSpec((tm, D_MODEL), row)] + [pl.BlockSpec(a.shape, fix) for a in w],
        out_specs=[pl.BlockSpec((tm, dk), row), pl.BlockSpec((tm, dk), row), pl.BlockSpec((tm, dv), row),
                   pl.BlockSpec((tm, dv), row), pl.BlockSpec((tm, dk), row)],
        out_shape=[jax.ShapeDtypeStruct((t, dk), BF16), jax.ShapeDtypeStruct((t, dk), BF16),
                   jax.ShapeDtypeStruct((t, dv), BF16), jax.ShapeDtypeStruct((t, dv), BF16),
                   jax.ShapeDtypeStruct((t, dk), F32)],
        compiler_params=_params("parallel"),
        name="gla_proj",
    )(x, *w)


def _gla_chunk_kernel(q_ref, k_ref, v_ref, g_ref, la_ref, s0_ref, gn_ref, o_ref, st_ref, s_sc, *, nchunk):
    ti = pl.program_id(1)

    @pl.when(ti == 0)
    def _():
        s_sc[...] = s0_ref[0]

    c = CHUNK
    nsub = c // SUB
    r64 = lax.broadcasted_iota(I32, (c, c), 0)
    c64 = lax.broadcasted_iota(I32, (c, c), 1)
    tri = (r64 >= c64).astype(BF16)
    ones = jnp.ones((GLA_DK, LANES), BF16)
    lane16 = lax.broadcasted_iota(I32, (SUB, LANES), 1)
    row_l = lax.broadcasted_iota(I32, (c, LANES), 0)
    lane_l = lax.broadcasted_iota(I32, (c, LANES), 1)
    diag_vis = (lane_l // SUB == row_l // SUB) & (lane_l <= row_l)
    gn = gn_ref[...]

    def chunk(ci, carry):
        r0 = pl.multiple_of(ci * c, c)
        rows = pl.ds(r0, c)
        for h in range(GLA_HEADS):
            dk = slice(h * GLA_DK, (h + 1) * GLA_DK)
            dv = slice(h * GLA_DV, (h + 1) * GLA_DV)
            la = la_ref[rows, dk]
            la_hi = la.astype(BF16)
            la_lo = (la - la_hi.astype(F32)).astype(BF16)
            b = _dot(tri, la_hi) + _dot(tri, la_lo)
            q = q_ref[rows, dk].astype(F32)
            k = k_ref[rows, dk].astype(F32)
            v = v_ref[rows, dv]
            s_t = s_sc[h]
            o = _dot_nt((q * jnp.exp(b)).astype(BF16), s_t.astype(BF16))
            parts = []
            for s in range(c):
                i0 = (s // SUB) * SUB
                parts.append(q[i0:i0 + SUB] * (k[s:s + 1] * jnp.exp(b[i0:i0 + SUB] - b[s:s + 1])))
            rs = _dot(jnp.concatenate(parts, 0).astype(BF16), ones)
            blocks = []
            for i in range(nsub):
                dm = jnp.zeros((SUB, LANES), F32)
                for ss in range(SUB):
                    s = SUB * i + ss
                    dm = jnp.where(lane16 == s, rs[s * SUB:(s + 1) * SUB], dm)
                blocks.append(dm)
            att = jnp.where(diag_vis, jnp.concatenate(blocks, 0), 0.0)[:, :c]
            rk = jnp.concatenate([jnp.broadcast_to(b[SUB * j + SUB - 1:SUB * j + SUB], (SUB, GLA_DK))
                                  for j in range(nsub)], 0)
            kt = (k * jnp.exp(rk - b)).astype(BF16)
            for j in range(nsub - 1):
                rj = b[SUB * j + SUB - 1:SUB * j + SUB]
                qj = (q * jnp.exp(jnp.minimum(b - rj, 0.0))).astype(BF16)
                att = jnp.where((r64 >= SUB * (j + 1)) & (c64 // SUB == j), _dot_nt(qj, kt), att)
            o = o + _dot(att.astype(BF16), v)
            bend = b[c - 1:c]
            kh = (k * jnp.exp(bend - b)).astype(BF16)
            s_sc[h] = s_t * jnp.exp(bend) + _dot_tn(v, kh)
            on = _rms(o, gn)
            o_ref[rows, dv] = (on * g_ref[rows, dv].astype(F32)).astype(BF16)
        return carry

    lax.fori_loop(0, nchunk, chunk, 0)

    @pl.when(ti == pl.num_programs(1) - 1)
    def _():
        st_ref[0] = s_sc[...]


def _gla_chunks(q, k, v, g, la, s0t, gn, *, nb, nt, tb, blk0):
    dk, dv = GLA_HEADS * GLA_DK, GLA_HEADS * GLA_DV
    row = lambda b, t: (blk0 + b * nt + t, 0)
    st = lambda b, t: (b, 0, 0, 0)
    return pl.pallas_call(
        functools.partial(_gla_chunk_kernel, nchunk=tb // CHUNK),
        grid=(nb, nt),
        in_specs=[pl.BlockSpec((tb, dk), row), pl.BlockSpec((tb, dk), row), pl.BlockSpec((tb, dv), row),
                  pl.BlockSpec((tb, dv), row), pl.BlockSpec((tb, dk), row),
                  pl.BlockSpec((1, GLA_HEADS, GLA_DV, GLA_DK), st), pl.BlockSpec(gn.shape, lambda b, t: (0, 0))],
        out_specs=[pl.BlockSpec((tb, dv), lambda b, t: (b * nt + t, 0)),
                   pl.BlockSpec((1, GLA_HEADS, GLA_DV, GLA_DK), st)],
        out_shape=[jax.ShapeDtypeStruct((nb * nt * tb, dv), BF16),
                   jax.ShapeDtypeStruct((nb, GLA_HEADS, GLA_DV, GLA_DK), F32)],
        scratch_shapes=[pltpu.VMEM((GLA_HEADS, GLA_DV, GLA_DK), F32)],
        compiler_params=_params("parallel", "arbitrary"),
        name="gla_chunks",
    )(q, k, v, g, la, s0t, gn)


def _router_kernel(y_ref, wh_ref, wl_ref, bias_ref, u_ref, idx_ref, w_ref, rank_ref, cnt_ref, carry_sc):
    @pl.when(pl.program_id(0) == 0)
    def _():
        carry_sc[...] = jnp.zeros(carry_sc.shape, F32)

    y = y_ref[...]
    tm = y.shape[0]
    yh = y.astype(BF16)
    yl = (y - yh.astype(F32)).astype(BF16)
    wh, wl = wh_ref[...], wl_ref[...]
    logit = _dot_nt(wh, yh) + (_dot_nt(wh, yl) + _dot_nt(wl, yh))
    score = _sigmoid(logit)
    sel = score + bias_ref[...]
    shp = (N_GROUPS, GROUP_SIZE, tm)
    x3 = sel.reshape(shp)
    sc3 = score.reshape(shp)
    eidx = lax.broadcasted_iota(I32, shp, 1).astype(F32)
    gidx = lax.broadcasted_iota(I32, shp, 0).astype(F32)
    eflat = gidx * GROUP_SIZE + eidx
    neg = -jnp.inf
    m1 = jnp.max(x3, 1, keepdims=True)
    i1 = jnp.min(jnp.where(x3 == m1, eidx, float(GROUP_SIZE)), 1, keepdims=True)
    m2 = jnp.max(jnp.where(eidx == i1, neg, x3), 1, keepdims=True)
    gs = m1 + m2
    g1 = lax.broadcasted_iota(I32, (N_GROUPS, 1, tm), 0).astype(F32)
    gsel = jnp.zeros((N_GROUPS, 1, tm), F32)
    for _ in range(TOPK_GROUPS):
        m = jnp.max(gs, 0, keepdims=True)
        gi = jnp.min(jnp.where(gs == m, g1, float(N_GROUPS)), 0, keepdims=True)
        pick = g1 == gi
        gsel = jnp.where(pick, 1.0, gsel)
        gs = jnp.where(pick, neg, gs)
    masked = jnp.where(gsel > 0.0, x3, neg)
    chosen = jnp.zeros(shp, F32)
    idxs, ws = [], []
    for _ in range(TOP_K):
        m = jnp.max(jnp.max(masked, 0, keepdims=True), 1, keepdims=True)
        ei = jnp.min(jnp.min(jnp.where(masked == m, eflat, float(N_EXPERTS)), 0, keepdims=True), 1, keepdims=True)
        pick = eflat == ei
        ws.append(jnp.sum(jnp.sum(jnp.where(pick, sc3, 0.0), 0, keepdims=True), 1, keepdims=True))
        idxs.append(ei)
        masked = jnp.where(pick, neg, masked)
        chosen = jnp.where(pick, 1.0, chosen)
    wsum = ws[0]
    for wk in ws[1:]:
        wsum = wsum + wk
    m2d = chosen.reshape(N_EXPERTS, tm)
    before = _dot(m2d.astype(BF16), u_ref[...]) + carry_sc[:, :1]
    b3 = before.reshape(shp)
    for kk in range(TOP_K):
        rk = jnp.sum(jnp.sum(jnp.where(eflat == idxs[kk], b3, 0.0), 0, keepdims=True), 1, keepdims=True)
        idx_ref[kk:kk + 1, :] = idxs[kk].reshape(1, tm).astype(I32)
        rank_ref[kk:kk + 1, :] = rk.reshape(1, tm).astype(I32)
        w_ref[kk:kk + 1, :] = (ws[kk] / wsum * ROUTED_SCALE).reshape(1, tm)
    carry_sc[...] = carry_sc[...] + jnp.sum(m2d, -1, keepdims=True)
    cnt_ref[...] = carry_sc[...]


def _router(y, wh, wl, bias, upper, tm):
    t = y.shape[0]
    col = lambda i: (0, i)
    fix = lambda i: (0, 0)
    return pl.pallas_call(
        _router_kernel,
        grid=(t // tm,),
        in_specs=[pl.BlockSpec((tm, D_MODEL), lambda i: (i, 0)), pl.BlockSpec(wh.shape, fix),
                  pl.BlockSpec(wl.shape, fix), pl.BlockSpec(bias.shape, fix), pl.BlockSpec(upper.shape, fix)],
        out_specs=[pl.BlockSpec((TOP_K, tm), col), pl.BlockSpec((TOP_K, tm), col), pl.BlockSpec((TOP_K, tm), col),
                   pl.BlockSpec((N_EXPERTS, LANES), fix)],
        out_shape=[jax.ShapeDtypeStruct((TOP_K, t), I32), jax.ShapeDtypeStruct((TOP_K, t), F32),
                   jax.ShapeDtypeStruct((TOP_K, t), I32), jax.ShapeDtypeStruct((N_EXPERTS, LANES), F32)],
        scratch_shapes=[pltpu.VMEM((N_EXPERTS, LANES), F32)],
        compiler_params=_params("arbitrary"),
        name="moe_router",
    )(y, wh, wl, bias, upper)


def _sc_mesh():
    return plsc.VectorSubcoreMesh(core_axis_name="core", subcore_axis_name="subcore")


def _sc_scatter_rows(xa, xb, pos, n_out):
    t, c = xa.shape
    k = pos.shape[0]
    nblk = t // SC_WINDOW
    idx = pos.reshape(1, k * t)
    out = jax.ShapeDtypeStruct((n_out, c), xa.dtype)

    @functools.partial(pl.kernel, out_type=(out, out), mesh=_sc_mesh(), scratch_types=[])
    def scatter(xa_hbm, xb_hbm, i_hbm, oa_hbm, ob_hbm):
        for x_hbm, o_hbm in ((xa_hbm, oa_hbm), (xb_hbm, ob_hbm)):
            def body(x_vmem, i_vmem, o_hbm=o_hbm):
                pltpu.sync_copy(x_vmem, o_hbm.at[i_vmem.at[0]])

            pltpu.emit_pipeline(
                body,
                grid=(k * nblk,),
                in_specs=[pl.BlockSpec((SC_WINDOW, c), lambda i: (i % nblk, 0)),
                          pl.BlockSpec((1, SC_WINDOW), lambda i: (0, i))],
                out_specs=[],
                core_axis_name=("core", "subcore"),
                dimension_semantics=(pltpu.PARALLEL,),
            )(x_hbm, i_hbm)

    return scatter(xa, xb, idx)


def _sc_gather_rows(ya, yb, pos):
    k, t = pos.shape
    c = ya.shape[1]
    n = k * t
    idx = pos.reshape(1, n)
    out = jax.ShapeDtypeStruct((n, c), ya.dtype)

    @functools.partial(pl.kernel, out_type=(out, out), mesh=_sc_mesh(), scratch_types=[])
    def gather(ya_hbm, yb_hbm, i_hbm, oa_hbm, ob_hbm):
        for y_hbm, o_hbm in ((ya_hbm, oa_hbm), (yb_hbm, ob_hbm)):
            def body(i_vmem, o_vmem, y_hbm=y_hbm):
                pltpu.sync_copy(y_hbm.at[i_vmem.at[0]], o_vmem)

            pltpu.emit_pipeline(
                body,
                grid=(n // SC_WINDOW,),
                in_specs=[pl.BlockSpec((1, SC_WINDOW), lambda i: (0, i))],
                out_specs=[pl.BlockSpec((SC_WINDOW, c), lambda i: (i, 0))],
                core_axis_name=("core", "subcore"),
                dimension_semantics=(pltpu.PARALLEL,),
            )(i_hbm, o_hbm)

    return gather(ya, yb, idx)


def _expert_kernel(te_ref, nu_ref, xa_ref, xb_ref, wgu_ref, wd_ref, ya_ref, yb_ref):
    @pl.when(pl.program_id(0) < nu_ref[0])
    def _():
        x = _unpack_halves(xa_ref[...], xb_ref[...]).astype(BF16)
        gu = _dot(x, wgu_ref[0])
        g, u = gu[:, :EXPERT_FF], gu[:, EXPERT_FF:]
        h = (g * _sigmoid(g) * u).astype(BF16)
        a, b = _pack_halves(_dot(h, wd_ref[0]))
        ya_ref[...] = a
        yb_ref[...] = b


def _experts(tile_expert, n_used, xa, xb, wgu, wd, tr):
    n_rows = xa.shape[0]
    row = lambda j, te, nu: (jnp.minimum(j, nu[0] - 1), 0)
    wmap = lambda j, te, nu: (te[jnp.minimum(j, nu[0] - 1)], 0, 0)
    grid_spec = pltpu.PrefetchScalarGridSpec(
        num_scalar_prefetch=2,
        grid=(n_rows // tr,),
        in_specs=[pl.BlockSpec((tr, HALF), row), pl.BlockSpec((tr, HALF), row),
                  pl.BlockSpec((1, D_MODEL, 2 * EXPERT_FF), wmap), pl.BlockSpec((1, EXPERT_FF, D_MODEL), wmap)],
        out_specs=[pl.BlockSpec((tr, HALF), row), pl.BlockSpec((tr, HALF), row)],
    )
    return pl.pallas_call(
        _expert_kernel,
        grid_spec=grid_spec,
        out_shape=[jax.ShapeDtypeStruct((n_rows, HALF), I32), jax.ShapeDtypeStruct((n_rows, HALF), I32)],
        compiler_params=_params("arbitrary"),
        name="moe_experts",
    )(tile_expert, n_used, xa, xb, wgu, wd)


def _combine_ln_kernel(x_ref, wt_ref, ya_ref, yb_ref, wsgu_ref, wsd_ref, g_ref, b_ref, y_ref):
    x = x_ref[...]
    gu = _dot(x.astype(BF16), wsgu_ref[...])
    g, u = gu[:, :EXPERT_FF], gu[:, EXPERT_FF:]
    acc = _dot((g * _sigmoid(g) * u).astype(BF16), wsd_ref[...])
    for k in range(TOP_K):
        acc = acc + wt_ref[:, k:k + 1] * _unpack_halves(ya_ref[k], yb_ref[k])
    y_ref[...] = _layer_norm(DEEPNORM_ALPHA * x + acc, g_ref[...], b_ref[...])


def _combine_ln(x, wt, yga, ygb, wsgu, wsd, g, b, tm):
    t = x.shape[0]
    row = lambda i: (i, 0)
    fix = lambda i: (0, 0)
    slot = lambda i: (0, i, 0)
    return pl.pallas_call(
        _combine_ln_kernel,
        grid=(t // tm,),
        in_specs=[pl.BlockSpec((tm, D_MODEL), row), pl.BlockSpec((tm, TOP_K), row),
                  pl.BlockSpec((TOP_K, tm, HALF), slot), pl.BlockSpec((TOP_K, tm, HALF), slot),
                  pl.BlockSpec(wsgu.shape, fix), pl.BlockSpec(wsd.shape, fix),
                  pl.BlockSpec(g.shape, fix), pl.BlockSpec(b.shape, fix)],
        out_specs=pl.BlockSpec((tm, D_MODEL), row),
        out_shape=jax.ShapeDtypeStruct((t, D_MODEL), F32),
        compiler_params=_params("parallel"),
        name="moe_combine_ln2",
    )(x, wt, yga.reshape(TOP_K, t, HALF), ygb.reshape(TOP_K, t, HALF), wsgu, wsd, g, b)


def _moe_layer(y, ya, yb, w, tm, tr, upper):
    wrh, wrl, rbias, wgu, wd, wsgu, wsd, g2, b2 = w
    t = y.shape[0]
    idx, wts, rank, cnt = _router(y, wrh, wrl, rbias, upper, tm)
    counts = cnt[:, 0].astype(I32)
    ntile = (counts + (tr - 1)) // tr
    tile_end = jnp.cumsum(ntile)
    offs = (tile_end - ntile) * tr
    pos = jnp.take(offs, idx) + rank
    n_tiles = (t * TOP_K) // tr + N_EXPERTS
    tile_expert = jnp.minimum(jnp.searchsorted(tile_end, jnp.arange(n_tiles, dtype=I32), side="right"),
                              N_EXPERTS - 1).astype(I32)
    n_used = tile_end[-1:].astype(I32)
    xsa, xsb = _sc_scatter_rows(ya, yb, pos, n_tiles * tr)
    ysa, ysb = _experts(tile_expert, n_used, xsa, xsb, wgu, wd, tr)
    yga, ygb = _sc_gather_rows(ysa, ysb, pos)
    return _combine_ln(y, wts.T, yga, ygb, wsgu, wsd, g2, b2, min(tm, 256))


def _rope_tables(seq, past, dec_seq, n_dec_rows):
    half = MLA_ROPE // 2
    inv = ROPE_THETA ** (-jnp.arange(half, dtype=F32) / half)
    pos = jnp.concatenate([jnp.arange(seq), past + (jnp.arange(n_dec_rows) % dec_seq)]).astype(F32)
    ang = pos[:, None] * inv[None, :]
    cos, sin = jnp.cos(ang), jnp.sin(ang)
    n = pos.shape[0]
    z = lambda w: jnp.zeros((n, w), F32)
    cos_t = jnp.concatenate([cos, cos, jnp.ones((n, MLA_NOPE), F32), z(HEAD_PAD - MLA_ROPE - MLA_NOPE)], 1)
    msin_t = jnp.concatenate([-sin, z(LANES - half)], 1)
    sin_t = jnp.concatenate([z(half), sin, z(LANES - 2 * half)], 1)
    return cos_t, msin_t, sin_t


def _mla_weights(w_dq, q_norm, w_uq, w_dkv, kv_norm, w_uk, w_uv, w_o):
    wq = (w_uq * MLA_SCALE).reshape(MLA_Q_LORA, MLA_HEADS, MLA_NOPE + MLA_ROPE)
    wq = jnp.concatenate([wq[..., MLA_NOPE:], wq[..., :MLA_NOPE],
                          jnp.zeros((MLA_Q_LORA, MLA_HEADS, HEAD_PAD - MLA_NOPE - MLA_ROPE), F32)], -1)
    wq = wq.reshape(MLA_Q_LORA, MLA_HEADS * HEAD_PAD).astype(BF16)
    wkc_lat = w_dkv[:, :MLA_KV_LORA].astype(BF16)
    wkp = jnp.pad(w_dkv[:, MLA_KV_LORA:], ((0, 0), (0, LANES - MLA_ROPE))).astype(BF16)
    wk = w_uk.reshape(MLA_KV_LORA, MLA_HEADS, MLA_NOPE)
    wk = jnp.concatenate([jnp.zeros((MLA_KV_LORA, MLA_HEADS, MLA_ROPE), F32), wk,
                          jnp.zeros((MLA_KV_LORA, MLA_HEADS, HEAD_PAD - MLA_NOPE - MLA_ROPE), F32)], -1)
    wk = wk.reshape(MLA_KV_LORA, MLA_HEADS * HEAD_PAD).astype(BF16)
    eye = jnp.concatenate([jnp.eye(MLA_ROPE, dtype=F32), jnp.zeros((MLA_ROPE, HEAD_PAD - MLA_ROPE), F32)], 1)
    wke = jnp.tile(eye, (1, MLA_HEADS)).astype(BF16)
    proj = (w_dq.astype(BF16), q_norm.reshape(1, -1), wq, wkc_lat, kv_norm.reshape(1, -1), wkp)
    expand = (wk, wke, w_uv.astype(BF16))
    return proj, expand, w_o.astype(BF16)


def _gla_weights(w_q, w_k, w_v, w_a1, w_a2, b_a, w_g):
    wa1 = jnp.pad(w_a1, ((0, 0), (0, LANES - GLA_GATE_RANK))).astype(BF16)
    wa2 = jnp.pad(w_a2, ((0, LANES - GLA_GATE_RANK), (0, 0))).astype(BF16)
    return (w_q.astype(BF16), w_k.astype(BF16), w_v.astype(BF16), w_g.astype(BF16), wa1, wa2, b_a.reshape(1, -1))


def _moe_weights(w_router, router_bias, w_gate, w_up, w_down, ws_gate, ws_up, ws_down, g2, b2):
    wr_t = w_router.T
    wrh = wr_t.astype(BF16)
    wrl = (wr_t - wrh.astype(F32)).astype(BF16)
    wgu = jnp.concatenate([w_gate, w_up], -1).astype(BF16)
    wsgu = jnp.concatenate([ws_gate, ws_up], -1).astype(BF16)
    return (wrh, wrl, router_bias.reshape(-1, 1), wgu, w_down.astype(BF16), wsgu, ws_down.astype(BF16),
            g2.reshape(1, -1), b2.reshape(1, -1))


def kernel(x_prompt, x_sample, cache_ckv, cache_kpe, state_gla, mla_w_dq, mla_q_norm, mla_w_uq, mla_w_dkv, mla_kv_norm, mla_w_uk, mla_w_uv, mla_w_o, gla_w_q, gla_w_k, gla_w_v, gla_w_a1, gla_w_a2, gla_b_a, gla_w_g, gla_g_norm, gla_w_o, ln1_g, ln1_b, ln2_g, ln2_b, moe_w_router, moe_router_bias, moe_w_gate, moe_w_up, moe_w_down, moe_ws_gate, moe_ws_up, moe_ws_down):
    nb, seq, _ = x_prompt.shape
    ndb, dec_seq, _ = x_sample.shape
    past = cache_ckv.shape[2]
    tp, ts = nb * seq, ndb * dec_seq
    t = tp + ts
    tm = 512 if (t % 512 == 0 and seq % 512 == 0 and ts <= 512) else 128
    tq = min(512, seq)
    tr = 512 if t >= 8192 else 128
    assert t % tm == 0 and tp % tm == 0 and ts % tm == 0 and tm % dec_seq == 0
    assert seq % tq == 0 and dec_seq == CHUNK and (past + dec_seq) % 16 == 0

    x = jnp.concatenate([x_prompt.reshape(tp, D_MODEL), x_sample.reshape(ts, D_MODEL)], 0)
    tabs = _rope_tables(seq, past, dec_seq, max(ts, tm))
    upper = (jnp.arange(tm)[:, None] < jnp.arange(tm)[None, :]).astype(BF16)
    new_ckv_p, new_kpe_p, new_gla_p, new_ckv_s, new_kpe_s, new_gla_s = [], [], [], [], [], []

    for i in range(DEPTH):
        j = i // 2
        if i % 2 == 0:
            proj_w, exp_w, wo = _mla_weights(mla_w_dq[j], mla_q_norm[j], mla_w_uq[j], mla_w_dkv[j], mla_kv_norm[j],
                                             mla_w_uk[j], mla_w_uv[j], mla_w_o[j])
            q, ckv, kpe = _mla_proj(x, tabs, proj_w, tm, tp // tm, seq // tm)
            k_p, v_p = _kv_expand(ckv, kpe, exp_w, tm, tp)
            c_all = jnp.concatenate([cache_ckv[j], ckv[tp:].reshape(ndb, dec_seq, -1)], 1).reshape(-1, MLA_KV_LORA)
            p_all = jnp.concatenate([cache_kpe[j], kpe[tp:].reshape(ndb, dec_seq, -1)], 1).reshape(-1, MLA_ROPE)
            n_all = c_all.shape[0]
            tkv = 512 if n_all % 512 == 0 else (past + dec_seq)
            k_s, v_s = _kv_expand(c_all, p_all, exp_w, tkv, n_all)
            o_p = _attention(q, k_p, v_p, nb=nb, nq=seq // tq, nk=seq // tq, tq=tq, tk=tq, q_blk0=0, causal=True)
            o_s = _attention(q, k_s, v_s, nb=ndb, nq=1, nk=1, tq=dec_seq, tk=past + dec_seq,
                             q_blk0=tp // dec_seq, causal=False)
            o = jnp.concatenate([o_p, o_s], 0)
            new_ckv_p.append(ckv[:tp].reshape(nb, seq, -1))
            new_kpe_p.append(kpe[:tp].reshape(nb, seq, -1))
            new_ckv_s.append(ckv[tp:].reshape(ndb, dec_seq, -1))
            new_kpe_s.append(kpe[tp:].reshape(ndb, dec_seq, -1))
        else:
            gw = _gla_weights(gla_w_q[j], gla_w_k[j], gla_w_v[j], gla_w_a1[j], gla_w_a2[j], gla_b_a[j], gla_w_g[j])
            wo = gla_w_o[j].astype(BF16)
            gn = gla_g_norm[j].reshape(1, -1)
            q, k, v, g, la = _gla_proj(x, gw, tm)
            zeros = jnp.zeros((nb, GLA_HEADS, GLA_DV, GLA_DK), F32)
            o_p, st_p = _gla_chunks(q, k, v, g, la, zeros, gn, nb=nb, nt=seq // tq, tb=tq, blk0=0)
            s0 = jnp.swapaxes(state_gla[j], -1, -2)
            o_s, st_s = _gla_chunks(q, k, v, g, la, s0, gn, nb=ndb, nt=1, tb=dec_seq, blk0=tp // dec_seq)
            o = jnp.concatenate([o_p, o_s], 0)
            new_gla_p.append(jnp.swapaxes(st_p, -1, -2))
            new_gla_s.append(jnp.swapaxes(st_s, -1, -2))
        y1, ya, yb = _proj_ln(o, x, wo, ln1_g[i].reshape(1, -1), ln1_b[i].reshape(1, -1), tm)
        mw = _moe_weights(moe_w_router[i], moe_router_bias[i], moe_w_gate[i], moe_w_up[i], moe_w_down[i],
                          moe_ws_gate[i], moe_ws_up[i], moe_ws_down[i], ln2_g[i], ln2_b[i])
        x = _moe_layer(y1, ya, yb, mw, tm, tr, upper)

    return (x[:tp].reshape(nb, seq, D_MODEL), x[tp:].reshape(ndb, dec_seq, D_MODEL),
            jnp.stack(new_ckv_p), jnp.stack(new_kpe_p), jnp.stack(new_gla_p),
            jnp.stack(new_ckv_s), jnp.stack(new_kpe_s), jnp.stack(new_gla_s))
```

```python
import functools

import jax
import jax.numpy as jnp
from jax import lax
from jax.experimental import pallas as pl
from jax.experimental.pallas import tpu as pltpu
from jax.experimental.pallas import tpu_sc as plsc

F32, BF16, I32 = jnp.float32, jnp.bfloat16, jnp.int32

D_MODEL = 1024
DEPTH = 4
CHUNK = 64
MLA_HEADS = 16
MLA_NOPE = 64
MLA_ROPE = 32
MLA_V = 64
MLA_Q_LORA = 384
MLA_KV_LORA = 256
MLA_SCALE = (MLA_NOPE + MLA_ROPE) ** -0.5
LOG2E = 1.4426950408889634
ROPE_THETA = 10000.0
GLA_HEADS = 4
GLA_DK = 128
GLA_DV = 256
GLA_GATE_RANK = 16
GLA_TAU = 16.0
N_EXPERTS = 64
TOP_K = 8
N_GROUPS = 8
GROUP_SIZE = N_EXPERTS // N_GROUPS
TOPK_GROUPS = 4
EXPERT_FF = 256
ROUTED_SCALE = 2.5
DEEPNORM_ALPHA = (2.0 * DEPTH) ** 0.25
NORM_EPS = 1e-5

LANES = 128
HEAD_PAD = 128
SUB = 16
SC_WINDOW = 128
HALF = D_MODEL // 4
VMEM_LIMIT = 48 * 1024 * 1024


def _dot(a, b):
    return jnp.dot(a, b, preferred_element_type=F32)


def _dot_nt(a, b):
    return lax.dot_general(a, b, (((1,), (1,)), ((), ())), preferred_element_type=F32)


def _dot_tn(a, b):
    return lax.dot_general(a, b, (((0,), (0,)), ((), ())), preferred_element_type=F32)


def _params(*sem):
    return pltpu.CompilerParams(dimension_semantics=sem, vmem_limit_bytes=VMEM_LIMIT)


def _sigmoid(x):
    return 1.0 / (1.0 + jnp.exp(-x))


def _rms(x, g):
    return x * lax.rsqrt(jnp.mean(x * x, -1, keepdims=True) + NORM_EPS) * g


def _layer_norm(z, g, b):
    mu = jnp.mean(z, -1, keepdims=True)
    zc = z - mu
    var = jnp.mean(zc * zc, -1, keepdims=True)
    return zc * lax.rsqrt(var + NORM_EPS) * g + b


def _rope128(x, cos, msin, sin):
    return x * cos + pltpu.roll(x, LANES - 16, 1) * msin + pltpu.roll(x, 16, 1) * sin


def _pack_halves(y):
    bits = lax.bitcast_convert_type(y.astype(BF16).astype(F32), I32)
    half = D_MODEL // 2
    p = (bits[:, :half] & jnp.int32(-65536)) | lax.shift_right_logical(bits[:, half:], jnp.int32(16))
    return p[:, :HALF], p[:, HALF:]


def _unpack_halves(a, b):
    p = jnp.concatenate([a, b], 1)
    hi = lax.bitcast_convert_type(p & jnp.int32(-65536), F32)
    lo = lax.bitcast_convert_type(lax.shift_left(p, jnp.int32(16)), F32)
    return jnp.concatenate([hi, lo], 1)


def _mla_proj_kernel(x_ref, cos_ref, msin_ref, sin_ref, wdq_ref, qn_ref, wuq_ref, wkc_ref, kvn_ref,
                     wkp_ref, q_ref, ckv_ref, kpe_ref):
    xb = x_ref[...].astype(BF16)
    cos, msin, sin = cos_ref[...], msin_ref[...], sin_ref[...]
    cq = _rms(_dot(xb, wdq_ref[...]), qn_ref[...]).astype(BF16)
    for hp in range(MLA_HEADS // 2):
        q2 = _dot(cq, wuq_ref[:, hp * 2 * HEAD_PAD:(hp + 1) * 2 * HEAD_PAD])
        for s in range(2):
            c0 = (2 * hp + s) * HEAD_PAD
            q_ref[:, c0:c0 + HEAD_PAD] = _rope128(q2[:, s * HEAD_PAD:(s + 1) * HEAD_PAD], cos, msin, sin).astype(BF16)
    ckv_ref[...] = _rms(_dot(xb, wkc_ref[...]), kvn_ref[...])
    kp = _rope128(_dot(xb, wkp_ref[...]), cos, msin, sin)
    kpe_ref[...] = kp[:, :MLA_ROPE]


def _mla_proj(x, tabs, w, tm, n_prompt_tiles, tiles_per_seq):
    t = x.shape[0]
    cos, msin, sin = tabs
    wdq, qn, wuq, wkc, kvn, wkp = w
    row = lambda i: (i, 0)
    fix = lambda i: (0, 0)
    tab = lambda i: (jnp.where(i < n_prompt_tiles, i % tiles_per_seq, tiles_per_seq), 0)
    full = lambda a: pl.BlockSpec(a.shape, fix)
    return pl.pallas_call(
        _mla_proj_kernel,
        grid=(t // tm,),
        in_specs=[pl.BlockSpec((tm, D_MODEL), row),
                  pl.BlockSpec((tm, LANES), tab), pl.BlockSpec((tm, LANES), tab), pl.BlockSpec((tm, LANES), tab),
                  full(wdq), full(qn), full(wuq), full(wkc), full(kvn), full(wkp)],
        out_specs=[pl.BlockSpec((tm, MLA_HEADS * HEAD_PAD), row),
                   pl.BlockSpec((tm, MLA_KV_LORA), row),
                   pl.BlockSpec((tm, MLA_ROPE), row)],
        out_shape=[jax.ShapeDtypeStruct((t, MLA_HEADS * HEAD_PAD), BF16),
                   jax.ShapeDtypeStruct((t, MLA_KV_LORA), F32),
                   jax.ShapeDtypeStruct((t, MLA_ROPE), F32)],
        compiler_params=_params("parallel"),
        name="mla_proj",
    )(x, cos, msin, sin, wdq, qn, wuq, wkc, kvn, wkp)


def _kv_expand_kernel(ckv_ref, kpe_ref, wkc_ref, wke_ref, wuv_ref, k_ref, v_ref):
    c = ckv_ref[...].astype(BF16)
    p = kpe_ref[...].astype(BF16)
    k_ref[...] = (_dot(c, wkc_ref[...]) + _dot(p, wke_ref[...])).astype(BF16)
    v_ref[...] = _dot(c, wuv_ref[...]).astype(BF16)


def _kv_expand(ckv, kpe, w, tm, n_rows):
    wkc, wke, wuv = w
    row = lambda i: (i, 0)
    fix = lambda i: (0, 0)
    full = lambda a: pl.BlockSpec(a.shape, fix)
    return pl.pallas_call(
        _kv_expand_kernel,
        grid=(n_rows // tm,),
        in_specs=[pl.BlockSpec((tm, MLA_KV_LORA), row), pl.BlockSpec((tm, MLA_ROPE), row),
                  full(wkc), full(wke), full(wuv)],
        out_specs=[pl.BlockSpec((tm, MLA_HEADS * HEAD_PAD), row),
                   pl.BlockSpec((tm, MLA_HEADS * MLA_V), row)],
        out_shape=[jax.ShapeDtypeStruct((n_rows, MLA_HEADS * HEAD_PAD), BF16),
                   jax.ShapeDtypeStruct((n_rows, MLA_HEADS * MLA_V), BF16)],
        compiler_params=_params("parallel"),
        name="mla_kv_expand",
    )(ckv, kpe, wkc, wke, wuv)


ATTN_HEADS = 4


def _attn_kernel(q_ref, k_ref, v_ref, o_ref, m_sc, l_sc, acc_sc, *, causal, tk):
    qi = pl.program_id(2)
    tq = q_ref.shape[0]
    m_sc[...] = jnp.full(m_sc.shape, -jnp.inf, F32)
    l_sc[...] = jnp.zeros(l_sc.shape, F32)
    acc_sc[...] = jnp.zeros(acc_sc.shape, F32)

    def block(ki, masked):
        rows = pl.ds(pl.multiple_of(ki * tk, tk), tk)
        vb = v_ref[rows, :]
        if masked:
            vis = (lax.broadcasted_iota(I32, (tq, tk), 1) // CHUNK) <= (lax.broadcasted_iota(I32, (tq, tk), 0) // CHUNK)
        for h in range(ATTN_HEADS):
            cols = slice(h * HEAD_PAD, (h + 1) * HEAD_PAD)
            s = _dot_nt(q_ref[:, cols], k_ref[rows, cols])
            if masked:
                s = jnp.where(vis, s, -jnp.inf)
            m_prev = m_sc[h]
            m_new = jnp.maximum(m_prev, jnp.max(s, -1, keepdims=True))
            a = jnp.exp2(m_prev - m_new)
            p = jnp.exp2(s - m_new[:, :1])
            l_sc[h] = a * l_sc[h] + jnp.sum(p, -1, keepdims=True)
            g = h // 2
            pv = _dot(p.astype(BF16), vb)[:, g * 2 * MLA_V:(g + 1) * 2 * MLA_V]
            acc_sc[h] = a * acc_sc[h] + pv
            m_sc[h] = m_new

    if causal:
        def body(ki, carry):
            block(ki, False)
            return carry
        lax.fori_loop(0, qi, body, 0)
        block(qi, True)
    else:
        block(0, False)

    lane = lax.broadcasted_iota(I32, (tq, 2 * MLA_V), 1)
    for g in range(ATTN_HEADS // 2):
        o = jnp.where(lane < MLA_V, acc_sc[2 * g] / l_sc[2 * g], acc_sc[2 * g + 1] / l_sc[2 * g + 1])
        o_ref[:, g * 2 * MLA_V:(g + 1) * 2 * MLA_V] = o.astype(BF16)


def _attention(q, k, v, *, nb, nq, tq, tk, kv_len, q_blk0, causal):
    hq = MLA_HEADS // ATTN_HEADS
    kvmap = lambda b, h, qi: (b, h)
    return pl.pallas_call(
        functools.partial(_attn_kernel, causal=causal, tk=tk),
        grid=(nb, hq, nq),
        in_specs=[pl.BlockSpec((tq, ATTN_HEADS * HEAD_PAD), lambda b, h, qi: (q_blk0 + b * nq + qi, h)),
                  pl.BlockSpec((kv_len, ATTN_HEADS * HEAD_PAD), kvmap),
                  pl.BlockSpec((kv_len, ATTN_HEADS * MLA_V), kvmap)],
        out_specs=pl.BlockSpec((tq, ATTN_HEADS * MLA_V), lambda b, h, qi: (b * nq + qi, h)),
        out_shape=jax.ShapeDtypeStruct((nb * nq * tq, MLA_HEADS * MLA_V), BF16),
        scratch_shapes=[pltpu.VMEM((ATTN_HEADS, tq, LANES), F32), pltpu.VMEM((ATTN_HEADS, tq, LANES), F32),
                        pltpu.VMEM((ATTN_HEADS, tq, 2 * MLA_V), F32)],
        compiler_params=_params("parallel", "parallel", "arbitrary"),
        name="mla_attention_causal" if causal else "mla_attention_full",
    )(q, k, v)


def _proj_ln_kernel(o_ref, x_ref, w_ref, g_ref, b_ref, y_ref, ya_ref, yb_ref):
    z = DEEPNORM_ALPHA * x_ref[...] + _dot(o_ref[...], w_ref[...])
    y = _layer_norm(z, g_ref[...], b_ref[...])
    y_ref[...] = y
    a, b = _pack_halves(y)
    ya_ref[...] = a
    yb_ref[...] = b


def _proj_ln(o, x, w, g, b, tm):
    t = x.shape[0]
    row = lambda i: (i, 0)
    fix = lambda i: (0, 0)
    return pl.pallas_call(
        _proj_ln_kernel,
        grid=(t // tm,),
        in_specs=[pl.BlockSpec((tm, D_MODEL), row), pl.BlockSpec((tm, D_MODEL), row),
                  pl.BlockSpec(w.shape, fix), pl.BlockSpec(g.shape, fix), pl.BlockSpec(b.shape, fix)],
        out_specs=[pl.BlockSpec((tm, D_MODEL), row), pl.BlockSpec((tm, HALF), row), pl.BlockSpec((tm, HALF), row)],
        out_shape=[jax.ShapeDtypeStruct((t, D_MODEL), F32), jax.ShapeDtypeStruct((t, HALF), I32),
                   jax.ShapeDtypeStruct((t, HALF), I32)],
        compiler_params=_params("parallel"),
        name="mixer_out_ln1",
    )(o, x, w, g, b)


def _gla_proj_kernel(x_ref, wq_ref, wk_ref, wv_ref, wg_ref, wa1_ref, wa2_ref, ba_ref,
                     q_ref, k_ref, v_ref, g_ref, la_ref):
    xb = x_ref[...].astype(BF16)
    q_ref[...] = (_dot(xb, wq_ref[...]) * (GLA_DK ** -0.5)).astype(BF16)
    k_ref[...] = _dot(xb, wk_ref[...]).astype(BF16)
    v_ref[...] = _dot(xb, wv_ref[...]).astype(BF16)
    gz = _dot(xb, wg_ref[...])
    g_ref[...] = (gz * _sigmoid(gz)).astype(BF16)
    z = _dot(_dot(xb, wa1_ref[...]).astype(BF16), wa2_ref[...]) + ba_ref[...]
    la_ref[...] = (jnp.minimum(z, 0.0) - jnp.log(1.0 + jnp.exp(-jnp.abs(z)))) * (1.0 / GLA_TAU)


def _gla_proj(x, w, tm):
    t = x.shape[0]
    row = lambda i: (i, 0)
    fix = lambda i: (0, 0)
    dk, dv = GLA_HEADS * GLA_DK, GLA_HEADS * GLA_DV
    return pl.pallas_call(
        _gla_proj_kernel,
        grid=(t // tm,),
        in_specs=[pl.BlockSpec((tm, D_MODEL), row)] + [pl.BlockSpec(a.shape, fix) for a in w],
        out_specs=[pl.BlockSpec((tm, dk), row), pl.BlockSpec((tm, dk), row), pl.BlockSpec((tm, dv), row),
                   pl.BlockSpec((tm, dv), row), pl.BlockSpec((tm, dk), row)],
        out_shape=[jax.ShapeDtypeStruct((t, dk), BF16), jax.ShapeDtypeStruct((t, dk), BF16),
                   jax.ShapeDtypeStruct((t, dv), BF16), jax.ShapeDtypeStruct((t, dv), BF16),
                   jax.ShapeDtypeStruct((t, dk), F32)],
        compiler_params=_params("parallel"),
        name="gla_proj",
    )(x, *w)


def _gla_chunk_kernel(q_ref, k_ref, v_ref, g_ref, la_ref, s0_ref, gn_ref, o_ref, st_ref, s_sc, *, nchunk):
    ti = pl.program_id(1)

    @pl.when(ti == 0)
    def _():
        s_sc[...] = s0_ref[0]

    c = CHUNK
    nsub = c // SUB
    r64 = lax.broadcasted_iota(I32, (c, c), 0)
    c64 = lax.broadcasted_iota(I32, (c, c), 1)
    tri = (r64 >= c64).astype(BF16)
    ones = jnp.ones((GLA_DK, LANES), BF16)
    lane16 = lax.broadcasted_iota(I32, (SUB, LANES), 1)
    row_l = lax.broadcasted_iota(I32, (c, LANES), 0)
    lane_l = lax.broadcasted_iota(I32, (c, LANES), 1)
    diag_vis = (lane_l // SUB == row_l // SUB) & (lane_l <= row_l)
    gn = gn_ref[...]

    def chunk(ci, carry):
        r0 = pl.multiple_of(ci * c, c)
        rows = pl.ds(r0, c)
        for h in range(GLA_HEADS):
            dk = slice(h * GLA_DK, (h + 1) * GLA_DK)
            dv = slice(h * GLA_DV, (h + 1) * GLA_DV)
            la = la_ref[rows, dk]
            la_hi = la.astype(BF16)
            la_lo = (la - la_hi.astype(F32)).astype(BF16)
            b = _dot(tri, la_hi) + _dot(tri, la_lo)
            q = q_ref[rows, dk].astype(F32)
            k = k_ref[rows, dk].astype(F32)
            v = v_ref[rows, dv]
            s_t = s_sc[h]
            o = _dot_nt((q * jnp.exp(b)).astype(BF16), s_t.astype(BF16))
            parts = []
            for s in range(c):
                i0 = (s // SUB) * SUB
                parts.append(q[i0:i0 + SUB] * (k[s:s + 1] * jnp.exp(b[i0:i0 + SUB] - b[s:s + 1])))
            rs = _dot(jnp.concatenate(parts, 0).astype(BF16), ones)
            blocks = []
            for i in range(nsub):
                dm = jnp.zeros((SUB, LANES), F32)
                for ss in range(SUB):
                    s = SUB * i + ss
                    dm = jnp.where(lane16 == s, rs[s * SUB:(s + 1) * SUB], dm)
                blocks.append(dm)
            att = jnp.where(diag_vis, jnp.concatenate(blocks, 0), 0.0)[:, :c]
            rk = jnp.concatenate([jnp.broadcast_to(b[SUB * j + SUB - 1:SUB * j + SUB], (SUB, GLA_DK))
                                  for j in range(nsub)], 0)
            kt = (k * jnp.exp(rk - b)).astype(BF16)
            for j in range(nsub - 1):
                rj = b[SUB * j + SUB - 1:SUB * j + SUB]
                qj = (q * jnp.exp(jnp.minimum(b - rj, 0.0))).astype(BF16)
                att = jnp.where((r64 >= SUB * (j + 1)) & (c64 // SUB == j), _dot_nt(qj, kt), att)
            o = o + _dot(att.astype(BF16), v)
            bend = b[c - 1:c]
            kh = (k * jnp.exp(bend - b)).astype(BF16)
            s_sc[h] = s_t * jnp.exp(bend) + _dot_tn(v, kh)
            on = _rms(o, gn)
            o_ref[rows, dv] = (on * g_ref[rows, dv].astype(F32)).astype(BF16)
        return carry

    lax.fori_loop(0, nchunk, chunk, 0)

    @pl.when(ti == pl.num_programs(1) - 1)
    def _():
        st_ref[0] = s_sc[...]


def _gla_chunks(q, k, v, g, la, s0t, gn, *, nb, nt, tb, blk0):
    dk, dv = GLA_HEADS * GLA_DK, GLA_HEADS * GLA_DV
    row = lambda b, t: (blk0 + b * nt + t, 0)
    st = lambda b, t: (b, 0, 0, 0)
    return pl.pallas_call(
        functools.partial(_gla_chunk_kernel, nchunk=tb // CHUNK),
        grid=(nb, nt),
        in_specs=[pl.BlockSpec((tb, dk), row), pl.BlockSpec((tb, dk), row), pl.BlockSpec((tb, dv), row),
                  pl.BlockSpec((tb, dv), row), pl.BlockSpec((tb, dk), row),
                  pl.BlockSpec((1, GLA_HEADS, GLA_DV, GLA_DK), st), pl.BlockSpec(gn.shape, lambda b, t: (0, 0))],
        out_specs=[pl.BlockSpec((tb, dv), lambda b, t: (b * nt + t, 0)),
                   pl.BlockSpec((1, GLA_HEADS, GLA_DV, GLA_DK), st)],
        out_shape=[jax.ShapeDtypeStruct((nb * nt * tb, dv), BF16),
                   jax.ShapeDtypeStruct((nb, GLA_HEADS, GLA_DV, GLA_DK), F32)],
        scratch_shapes=[pltpu.VMEM((GLA_HEADS, GLA_DV, GLA_DK), F32)],
        compiler_params=_params("parallel", "arbitrary"),
        name="gla_chunks",
    )(q, k, v, g, la, s0t, gn)


def _router_kernel(y_ref, wh_ref, wl_ref, bias_ref, u_ref, idx_ref, w_ref, rank_ref, cnt_ref, carry_sc):
    @pl.when(pl.program_id(0) == 0)
    def _():
        carry_sc[...] = jnp.zeros(carry_sc.shape, F32)

    y = y_ref[...]
    tm = y.shape[0]
    yh = y.astype(BF16)
    yl = (y - yh.astype(F32)).astype(BF16)
    wh, wl = wh_ref[...], wl_ref[...]
    logit = _dot_nt(wh, yh) + (_dot_nt(wh, yl) + _dot_nt(wl, yh))
    score = _sigmoid(logit)
    sel = score + bias_ref[...]
    shp = (N_GROUPS, GROUP_SIZE, tm)
    x3 = sel.reshape(shp)
    sc3 = score.reshape(shp)
    eidx = lax.broadcasted_iota(I32, shp, 1).astype(F32)
    gidx = lax.broadcasted_iota(I32, shp, 0).astype(F32)
    eflat = gidx * GROUP_SIZE + eidx
    neg = -jnp.inf
    m1 = jnp.max(x3, 1, keepdims=True)
    i1 = jnp.min(jnp.where(x3 == m1, eidx, float(GROUP_SIZE)), 1, keepdims=True)
    m2 = jnp.max(jnp.where(eidx == i1, neg, x3), 1, keepdims=True)
    gs = m1 + m2
    g1 = lax.broadcasted_iota(I32, (N_GROUPS, 1, tm), 0).astype(F32)
    gsel = jnp.zeros((N_GROUPS, 1, tm), F32)
    for _ in range(TOPK_GROUPS):
        m = jnp.max(gs, 0, keepdims=True)
        gi = jnp.min(jnp.where(gs == m, g1, float(N_GROUPS)), 0, keepdims=True)
        pick = g1 == gi
        gsel = jnp.where(pick, 1.0, gsel)
        gs = jnp.where(pick, neg, gs)
    masked = jnp.where(gsel > 0.0, x3, neg)
    chosen = jnp.zeros(shp, F32)
    idxs, ws = [], []
    for _ in range(TOP_K):
        m = jnp.max(jnp.max(masked, 0, keepdims=True), 1, keepdims=True)
        ei = jnp.min(jnp.min(jnp.where(masked == m, eflat, float(N_EXPERTS)), 0, keepdims=True), 1, keepdims=True)
        pick = eflat == ei
        ws.append(jnp.sum(jnp.sum(jnp.where(pick, sc3, 0.0), 0, keepdims=True), 1, keepdims=True))
        idxs.append(ei)
        masked = jnp.where(pick, neg, masked)
        chosen = jnp.where(pick, 1.0, chosen)
    wsum = ws[0]
    for wk in ws[1:]:
        wsum = wsum + wk
    m2d = chosen.reshape(N_EXPERTS, tm)
    before = _dot(m2d.astype(BF16), u_ref[...]) + carry_sc[:, :1]
    b3 = before.reshape(shp)
    for kk in range(TOP_K):
        rk = jnp.sum(jnp.sum(jnp.where(eflat == idxs[kk], b3, 0.0), 0, keepdims=True), 1, keepdims=True)
        idx_ref[kk:kk + 1, :] = idxs[kk].reshape(1, tm).astype(I32)
        rank_ref[kk:kk + 1, :] = rk.reshape(1, tm).astype(I32)
        w_ref[kk:kk + 1, :] = (ws[kk] / wsum * ROUTED_SCALE).reshape(1, tm)
    carry_sc[...] = carry_sc[...] + jnp.sum(m2d, -1, keepdims=True)
    cnt_ref[...] = carry_sc[...]


def _router(y, wh, wl, bias, upper, tm):
    t = y.shape[0]
    col = lambda i: (0, i)
    fix = lambda i: (0, 0)
    return pl.pallas_call(
        _router_kernel,
        grid=(t // tm,),
        in_specs=[pl.BlockSpec((tm, D_MODEL), lambda i: (i, 0)), pl.BlockSpec(wh.shape, fix),
                  pl.BlockSpec(wl.shape, fix), pl.BlockSpec(bias.shape, fix), pl.BlockSpec(upper.shape, fix)],
        out_specs=[pl.BlockSpec((TOP_K, tm), col), pl.BlockSpec((TOP_K, tm), col), pl.BlockSpec((TOP_K, tm), col),
                   pl.BlockSpec((N_EXPERTS, LANES), fix)],
        out_shape=[jax.ShapeDtypeStruct((TOP_K, t), I32), jax.ShapeDtypeStruct((TOP_K, t), F32),
                   jax.ShapeDtypeStruct((TOP_K, t), I32), jax.ShapeDtypeStruct((N_EXPERTS, LANES), F32)],
        scratch_shapes=[pltpu.VMEM((N_EXPERTS, LANES), F32)],
        compiler_params=_params("arbitrary"),
        name="moe_router",
    )(y, wh, wl, bias, upper)


def _sc_mesh():
    return plsc.VectorSubcoreMesh(core_axis_name="core", subcore_axis_name="subcore")


def _sc_scatter_rows(xa, xb, pos, n_out):
    t, c = xa.shape
    k = pos.shape[0]
    nblk = t // SC_WINDOW
    idx = pos.reshape(1, k * t)
    out = jax.ShapeDtypeStruct((n_out, c), xa.dtype)

    @functools.partial(pl.kernel, out_type=(out, out), mesh=_sc_mesh(), scratch_types=[])
    def scatter(xa_hbm, xb_hbm, i_hbm, oa_hbm, ob_hbm):
        for x_hbm, o_hbm in ((xa_hbm, oa_hbm), (xb_hbm, ob_hbm)):
            def body(x_vmem, i_vmem, o_hbm=o_hbm):
                pltpu.sync_copy(x_vmem, o_hbm.at[i_vmem.at[0]])

            pltpu.emit_pipeline(
                body,
                grid=(k * nblk,),
                in_specs=[pl.BlockSpec((SC_WINDOW, c), lambda i: (i % nblk, 0)),
                          pl.BlockSpec((1, SC_WINDOW), lambda i: (0, i))],
                out_specs=[],
                core_axis_name=("core", "subcore"),
                dimension_semantics=(pltpu.PARALLEL,),
                trace_scopes=False,
            )(x_hbm, i_hbm)

    return scatter(xa, xb, idx)


def _sc_gather_rows(ya, yb, pos):
    k, t = pos.shape
    c = ya.shape[1]
    n = k * t
    idx = pos.reshape(1, n)
    out = jax.ShapeDtypeStruct((n, c), ya.dtype)

    @functools.partial(pl.kernel, out_type=(out, out), mesh=_sc_mesh(), scratch_types=[])
    def gather(ya_hbm, yb_hbm, i_hbm, oa_hbm, ob_hbm):
        for y_hbm, o_hbm in ((ya_hbm, oa_hbm), (yb_hbm, ob_hbm)):
            def body(i_vmem, o_vmem, y_hbm=y_hbm):
                pltpu.sync_copy(y_hbm.at[i_vmem.at[0]], o_vmem)

            pltpu.emit_pipeline(
                body,
                grid=(n // SC_WINDOW,),
                in_specs=[pl.BlockSpec((1, SC_WINDOW), lambda i: (0, i))],
                out_specs=[pl.BlockSpec((SC_WINDOW, c), lambda i: (i, 0))],
                core_axis_name=("core", "subcore"),
                dimension_semantics=(pltpu.PARALLEL,),
                trace_scopes=False,
            )(i_hbm, o_hbm)

    return gather(ya, yb, idx)


def _expert_kernel(te_ref, nu_ref, xa_ref, xb_ref, wg_ref, wu_ref, wd_ref, ya_ref, yb_ref, wgu_sc, wd_sc):
    j = pl.program_id(0)

    @pl.when(j < nu_ref[0])
    def _():
        @pl.when((j == 0) | (te_ref[j] != te_ref[jnp.maximum(j - 1, 0)]))
        def _():
            wgu_sc[:, :EXPERT_FF] = wg_ref[0].astype(BF16)
            wgu_sc[:, EXPERT_FF:] = wu_ref[0].astype(BF16)
            wd_sc[...] = wd_ref[0].astype(BF16)

        x = _unpack_halves(xa_ref[...], xb_ref[...]).astype(BF16)
        gu = _dot(x, wgu_sc[...])
        g, u = gu[:, :EXPERT_FF], gu[:, EXPERT_FF:]
        h = (g * _sigmoid(g) * u).astype(BF16)
        a, b = _pack_halves(_dot(h, wd_sc[...]))
        ya_ref[...] = a
        yb_ref[...] = b


def _experts(tile_expert, n_used, xa, xb, wg, wu, wd, tr):
    n_rows = xa.shape[0]
    row = lambda j, te, nu: (jnp.minimum(j, nu[0] - 1), 0)
    wmap = lambda j, te, nu: (te[jnp.minimum(j, nu[0] - 1)], 0, 0)
    grid_spec = pltpu.PrefetchScalarGridSpec(
        num_scalar_prefetch=2,
        grid=(n_rows // tr,),
        in_specs=[pl.BlockSpec((tr, HALF), row), pl.BlockSpec((tr, HALF), row),
                  pl.BlockSpec((1, D_MODEL, EXPERT_FF), wmap), pl.BlockSpec((1, D_MODEL, EXPERT_FF), wmap),
                  pl.BlockSpec((1, EXPERT_FF, D_MODEL), wmap)],
        out_specs=[pl.BlockSpec((tr, HALF), row), pl.BlockSpec((tr, HALF), row)],
        scratch_shapes=[pltpu.VMEM((D_MODEL, 2 * EXPERT_FF), BF16), pltpu.VMEM((EXPERT_FF, D_MODEL), BF16)],
    )
    return pl.pallas_call(
        _expert_kernel,
        grid_spec=grid_spec,
        out_shape=[jax.ShapeDtypeStruct((n_rows, HALF), I32), jax.ShapeDtypeStruct((n_rows, HALF), I32)],
        compiler_params=_params("arbitrary"),
        name="moe_experts",
    )(tile_expert, n_used, xa, xb, wg, wu, wd)


def _combine_ln_kernel(x_ref, wt_ref, ya_ref, yb_ref, wsgu_ref, wsd_ref, g_ref, b_ref, y_ref):
    x = x_ref[...]
    gu = _dot(x.astype(BF16), wsgu_ref[...])
    g, u = gu[:, :EXPERT_FF], gu[:, EXPERT_FF:]
    acc = _dot((g * _sigmoid(g) * u).astype(BF16), wsd_ref[...])
    for k in range(TOP_K):
        acc = acc + wt_ref[:, k:k + 1] * _unpack_halves(ya_ref[k], yb_ref[k])
    y_ref[...] = _layer_norm(DEEPNORM_ALPHA * x + acc, g_ref[...], b_ref[...])


def _combine_ln(x, wt, yga, ygb, wsgu, wsd, g, b, tm):
    t = x.shape[0]
    row = lambda i: (i, 0)
    fix = lambda i: (0, 0)
    slot = lambda i: (0, i, 0)
    return pl.pallas_call(
        _combine_ln_kernel,
        grid=(t // tm,),
        in_specs=[pl.BlockSpec((tm, D_MODEL), row), pl.BlockSpec((tm, TOP_K), row),
                  pl.BlockSpec((TOP_K, tm, HALF), slot), pl.BlockSpec((TOP_K, tm, HALF), slot),
                  pl.BlockSpec(wsgu.shape, fix), pl.BlockSpec(wsd.shape, fix),
                  pl.BlockSpec(g.shape, fix), pl.BlockSpec(b.shape, fix)],
        out_specs=pl.BlockSpec((tm, D_MODEL), row),
        out_shape=jax.ShapeDtypeStruct((t, D_MODEL), F32),
        compiler_params=_params("parallel"),
        name="moe_combine_ln2",
    )(x, wt, yga.reshape(TOP_K, t, HALF), ygb.reshape(TOP_K, t, HALF), wsgu, wsd, g, b)


def _moe_layer(y, ya, yb, w, tm, tr, upper):
    wrh, wrl, rbias, wg, wu, wd, wsgu, wsd, g2, b2 = w
    t = y.shape[0]
    idx, wts, rank, cnt = _router(y, wrh, wrl, rbias, upper, tm)
    counts = cnt[:, 0].astype(I32)
    ntile = (counts + (tr - 1)) // tr
    tile_end = jnp.cumsum(ntile)
    offs = (tile_end - ntile) * tr
    pos = rank
    for e in range(1, N_EXPERTS):
        pos = pos + jnp.where(idx == e, offs[e], 0)
    n_tiles = (t * TOP_K) // tr + N_EXPERTS
    tile_expert = jnp.minimum(jnp.sum((tile_end[None, :] <= jnp.arange(n_tiles, dtype=I32)[:, None]).astype(I32), 1),
                              N_EXPERTS - 1)
    n_used = tile_end[-1:].astype(I32)
    xsa, xsb = _sc_scatter_rows(ya, yb, pos, n_tiles * tr)
    ysa, ysb = _experts(tile_expert, n_used, xsa, xsb, wg, wu, wd, tr)
    yga, ygb = _sc_gather_rows(ysa, ysb, pos)
    return _combine_ln(y, wts.T, yga, ygb, wsgu, wsd, g2, b2, min(tm, 256))


def _rope_tables(seq, past, dec_seq, n_dec_rows):
    half = MLA_ROPE // 2
    inv = ROPE_THETA ** (-jnp.arange(half, dtype=F32) / half)
    pos = jnp.concatenate([jnp.arange(seq), past + (jnp.arange(n_dec_rows) % dec_seq)]).astype(F32)
    ang = pos[:, None] * inv[None, :]
    cos, sin = jnp.cos(ang), jnp.sin(ang)
    n = pos.shape[0]
    z = lambda w: jnp.zeros((n, w), F32)
    cos_t = jnp.concatenate([cos, cos, jnp.ones((n, MLA_NOPE), F32), z(HEAD_PAD - MLA_ROPE - MLA_NOPE)], 1)
    msin_t = jnp.concatenate([-sin, z(LANES - half)], 1)
    sin_t = jnp.concatenate([z(half), sin, z(LANES - 2 * half)], 1)
    return cos_t, msin_t, sin_t


def _mla_weights(w_dq, q_norm, w_uq, w_dkv, kv_norm, w_uk, w_uv, w_o):
    wq = (w_uq * (MLA_SCALE * LOG2E)).reshape(MLA_Q_LORA, MLA_HEADS, MLA_NOPE + MLA_ROPE)
    wq = jnp.concatenate([wq[..., MLA_NOPE:], wq[..., :MLA_NOPE],
                          jnp.zeros((MLA_Q_LORA, MLA_HEADS, HEAD_PAD - MLA_NOPE - MLA_ROPE), F32)], -1)
    wq = wq.reshape(MLA_Q_LORA, MLA_HEADS * HEAD_PAD).astype(BF16)
    wkc_lat = w_dkv[:, :MLA_KV_LORA].astype(BF16)
    wkp = jnp.pad(w_dkv[:, MLA_KV_LORA:], ((0, 0), (0, LANES - MLA_ROPE))).astype(BF16)
    wk = w_uk.reshape(MLA_KV_LORA, MLA_HEADS, MLA_NOPE)
    wk = jnp.concatenate([jnp.zeros((MLA_KV_LORA, MLA_HEADS, MLA_ROPE), F32), wk,
                          jnp.zeros((MLA_KV_LORA, MLA_HEADS, HEAD_PAD - MLA_NOPE - MLA_ROPE), F32)], -1)
    wk = wk.reshape(MLA_KV_LORA, MLA_HEADS * HEAD_PAD).astype(BF16)
    eye = jnp.concatenate([jnp.eye(MLA_ROPE, dtype=F32), jnp.zeros((MLA_ROPE, HEAD_PAD - MLA_ROPE), F32)], 1)
    wke = jnp.tile(eye, (1, MLA_HEADS)).astype(BF16)
    proj = (w_dq.astype(BF16), q_norm.reshape(1, -1), wq, wkc_lat, kv_norm.reshape(1, -1), wkp)
    expand = (wk, wke, w_uv.astype(BF16))
    return proj, expand, w_o.astype(BF16)


def _gla_weights(w_q, w_k, w_v, w_a1, w_a2, b_a, w_g):
    wa1 = jnp.pad(w_a1, ((0, 0), (0, LANES - GLA_GATE_RANK))).astype(BF16)
    wa2 = jnp.pad(w_a2, ((0, LANES - GLA_GATE_RANK), (0, 0))).astype(BF16)
    return (w_q.astype(BF16), w_k.astype(BF16), w_v.astype(BF16), w_g.astype(BF16), wa1, wa2, b_a.reshape(1, -1))


def _moe_weights(w_router, router_bias, w_gate, w_up, w_down, ws_gate, ws_up, ws_down, g2, b2):
    wr_t = w_router.T
    wrh = wr_t.astype(BF16)
    wrl = (wr_t - wrh.astype(F32)).astype(BF16)
    wsgu = jnp.concatenate([ws_gate, ws_up], -1).astype(BF16)
    return (wrh, wrl, router_bias.reshape(-1, 1), w_gate, w_up, w_down, wsgu, ws_down.astype(BF16),
            g2.reshape(1, -1), b2.reshape(1, -1))


def kernel(x_prompt, x_sample, cache_ckv, cache_kpe, state_gla, mla_w_dq, mla_q_norm, mla_w_uq, mla_w_dkv, mla_kv_norm, mla_w_uk, mla_w_uv, mla_w_o, gla_w_q, gla_w_k, gla_w_v, gla_w_a1, gla_w_a2, gla_b_a, gla_w_g, gla_g_norm, gla_w_o, ln1_g, ln1_b, ln2_g, ln2_b, moe_w_router, moe_router_bias, moe_w_gate, moe_w_up, moe_w_down, moe_ws_gate, moe_ws_up, moe_ws_down):
    nb, seq, _ = x_prompt.shape
    ndb, dec_seq, _ = x_sample.shape
    past = cache_ckv.shape[2]
    tp, ts = nb * seq, ndb * dec_seq
    t = tp + ts
    tm = 512 if (t % 512 == 0 and seq % 512 == 0 and ts <= 512) else 128
    tq = min(512, seq)
    tr = 512 if t >= 8192 else 128
    assert t % tm == 0 and tp % tm == 0 and ts % tm == 0 and tm % dec_seq == 0
    assert seq % tq == 0 and dec_seq == CHUNK and (past + dec_seq) % 16 == 0

    x = jnp.concatenate([x_prompt.reshape(tp, D_MODEL), x_sample.reshape(ts, D_MODEL)], 0)
    tabs = _rope_tables(seq, past, dec_seq, max(ts, tm))
    upper = (jnp.arange(tm)[:, None] < jnp.arange(tm)[None, :]).astype(BF16)
    new_ckv_p, new_kpe_p, new_gla_p, new_ckv_s, new_kpe_s, new_gla_s = [], [], [], [], [], []

    for i in range(DEPTH):
        j = i // 2
        if i % 2 == 0:
            proj_w, exp_w, wo = _mla_weights(mla_w_dq[j], mla_q_norm[j], mla_w_uq[j], mla_w_dkv[j], mla_kv_norm[j],
                                             mla_w_uk[j], mla_w_uv[j], mla_w_o[j])
            q, ckv, kpe = _mla_proj(x, tabs, proj_w, tm, tp // tm, seq // tm)
            k_p, v_p = _kv_expand(ckv, kpe, exp_w, tm, tp)
            c_all = jnp.concatenate([cache_ckv[j], ckv[tp:].reshape(ndb, dec_seq, -1)], 1).reshape(-1, MLA_KV_LORA)
            p_all = jnp.concatenate([cache_kpe[j], kpe[tp:].reshape(ndb, dec_seq, -1)], 1).reshape(-1, MLA_ROPE)
            n_all = c_all.shape[0]
            tkv = 512 if n_all % 512 == 0 else (past + dec_seq)
            k_s, v_s = _kv_expand(c_all, p_all, exp_w, tkv, n_all)
            o_p = _attention(q, k_p, v_p, nb=nb, nq=seq // tq, tq=tq, tk=tq, kv_len=seq, q_blk0=0, causal=True)
            o_s = _attention(q, k_s, v_s, nb=ndb, nq=1, tq=dec_seq, tk=past + dec_seq, kv_len=past + dec_seq,
                             q_blk0=tp // dec_seq, causal=False)
            o = jnp.concatenate([o_p, o_s], 0)
            new_ckv_p.append(ckv[:tp].reshape(nb, seq, -1))
            new_kpe_p.append(kpe[:tp].reshape(nb, seq, -1))
            new_ckv_s.append(ckv[tp:].reshape(ndb, dec_seq, -1))
            new_kpe_s.append(kpe[tp:].reshape(ndb, dec_seq, -1))
        else:
            gw = _gla_weights(gla_w_q[j], gla_w_k[j], gla_w_v[j], gla_w_a1[j], gla_w_a2[j], gla_b_a[j], gla_w_g[j])
            wo = gla_w_o[j].astype(BF16)
            gn = gla_g_norm[j].reshape(1, -1)
            q, k, v, g, la = _gla_proj(x, gw, tm)
            zeros = jnp.zeros((nb, GLA_HEADS, GLA_DV, GLA_DK), F32)
            o_p, st_p = _gla_chunks(q, k, v, g, la, zeros, gn, nb=nb, nt=seq // tq, tb=tq, blk0=0)
            s0 = jnp.swapaxes(state_gla[j], -1, -2)
            o_s, st_s = _gla_chunks(q, k, v, g, la, s0, gn, nb=ndb, nt=1, tb=dec_seq, blk0=tp // dec_seq)
            o = jnp.concatenate([o_p, o_s], 0)
            new_gla_p.append(jnp.swapaxes(st_p, -1, -2))
            new_gla_s.append(jnp.swapaxes(st_s, -1, -2))
        y1, ya, yb = _proj_ln(o, x, wo, ln1_g[i].reshape(1, -1), ln1_b[i].reshape(1, -1), tm)
        mw = _moe_weights(moe_w_router[i], moe_router_bias[i], moe_w_gate[i], moe_w_up[i], moe_w_down[i],
                          moe_ws_gate[i], moe_ws_up[i], moe_ws_down[i], ln2_g[i], ln2_b[i])
        x = _moe_layer(y1, ya, yb, mw, tm, tr, upper)

    return (x[:tp].reshape(nb, seq, D_MODEL), x[tp:].reshape(ndb, dec_seq, D_MODEL),
            jnp.stack(new_ckv_p), jnp.stack(new_kpe_p), jnp.stack(new_gla_p),
            jnp.stack(new_ckv_s), jnp.stack(new_kpe_s), jnp.stack(new_gla_s))
```

```python
import functools

import jax
import jax.numpy as jnp
from jax import lax
from jax.experimental import pallas as pl
from jax.experimental.pallas import tpu as pltpu
from jax.experimental.pallas import tpu_sc as plsc

F32, BF16, I32 = jnp.float32, jnp.bfloat16, jnp.int32

D_MODEL = 1024
DEPTH = 4
CHUNK = 64
MLA_HEADS = 16
MLA_NOPE = 64
MLA_ROPE = 32
MLA_V = 64
MLA_Q_LORA = 384
MLA_KV_LORA = 256
MLA_SCALE = (MLA_NOPE + MLA_ROPE) ** -0.5
LOG2E = 1.4426950408889634
ROPE_THETA = 10000.0
GLA_HEADS = 4
GLA_DK = 128
GLA_DV = 256
GLA_GATE_RANK = 16
GLA_TAU = 16.0
N_EXPERTS = 64
TOP_K = 8
N_GROUPS = 8
GROUP_SIZE = N_EXPERTS // N_GROUPS
TOPK_GROUPS = 4
EXPERT_FF = 256
ROUTED_SCALE = 2.5
DEEPNORM_ALPHA = (2.0 * DEPTH) ** 0.25
NORM_EPS = 1e-5

LANES = 128
HEAD_PAD = 128
SUB = 16
SC_WINDOW = 128
HALF = D_MODEL // 4
VMEM_LIMIT = 48 * 1024 * 1024


def _dot(a, b):
    return jnp.dot(a, b, preferred_element_type=F32)


def _dot_nt(a, b):
    return lax.dot_general(a, b, (((1,), (1,)), ((), ())), preferred_element_type=F32)


def _dot_tn(a, b):
    return lax.dot_general(a, b, (((0,), (0,)), ((), ())), preferred_element_type=F32)


def _params(*sem):
    return pltpu.CompilerParams(dimension_semantics=sem, vmem_limit_bytes=VMEM_LIMIT)


def _sigmoid(x):
    return 1.0 / (1.0 + jnp.exp(-x))


def _rms(x, g):
    return x * lax.rsqrt(jnp.mean(x * x, -1, keepdims=True) + NORM_EPS) * g


def _layer_norm(z, g, b):
    mu = jnp.mean(z, -1, keepdims=True)
    zc = z - mu
    var = jnp.mean(zc * zc, -1, keepdims=True)
    return zc * lax.rsqrt(var + NORM_EPS) * g + b


def _rope128(x, cos, msin, sin):
    return x * cos + pltpu.roll(x, LANES - 16, 1) * msin + pltpu.roll(x, 16, 1) * sin


def _pack_halves(y):
    bits = lax.bitcast_convert_type(y.astype(BF16).astype(F32), I32)
    half = D_MODEL // 2
    p = (bits[:, :half] & jnp.int32(-65536)) | lax.shift_right_logical(bits[:, half:], jnp.int32(16))
    return p[:, :HALF], p[:, HALF:]


def _unpack_halves(a, b):
    p = jnp.concatenate([a, b], 1)
    hi = lax.bitcast_convert_type(p & jnp.int32(-65536), F32)
    lo = lax.bitcast_convert_type(lax.shift_left(p, jnp.int32(16)), F32)
    return jnp.concatenate([hi, lo], 1)


def _mla_proj_kernel(x_ref, cos_ref, msin_ref, sin_ref, wdq_ref, qn_ref, wuq_ref, wkc_ref, kvn_ref,
                     wkp_ref, q_ref, ckv_ref, kpe_ref):
    xb = x_ref[...].astype(BF16)
    cos, msin, sin = cos_ref[...], msin_ref[...], sin_ref[...]
    cq = _rms(_dot(xb, wdq_ref[...]), qn_ref[...]).astype(BF16)
    for hp in range(MLA_HEADS // 2):
        q2 = _dot(cq, wuq_ref[:, hp * 2 * HEAD_PAD:(hp + 1) * 2 * HEAD_PAD])
        for s in range(2):
            c0 = (2 * hp + s) * HEAD_PAD
            q_ref[:, c0:c0 + HEAD_PAD] = _rope128(q2[:, s * HEAD_PAD:(s + 1) * HEAD_PAD], cos, msin, sin).astype(BF16)
    ckv_ref[...] = _rms(_dot(xb, wkc_ref[...]), kvn_ref[...])
    kp = _rope128(_dot(xb, wkp_ref[...]), cos, msin, sin)
    kpe_ref[...] = kp[:, :MLA_ROPE]


def _mla_proj(x, tabs, w, tm, n_prompt_tiles, tiles_per_seq):
    t = x.shape[0]
    cos, msin, sin = tabs
    wdq, qn, wuq, wkc, kvn, wkp = w
    row = lambda i: (i, 0)
    fix = lambda i: (0, 0)
    tab = lambda i: (jnp.where(i < n_prompt_tiles, i % tiles_per_seq, tiles_per_seq), 0)
    full = lambda a: pl.BlockSpec(a.shape, fix)
    return pl.pallas_call(
        _mla_proj_kernel,
        grid=(t // tm,),
        in_specs=[pl.BlockSpec((tm, D_MODEL), row),
                  pl.BlockSpec((tm, LANES), tab), pl.BlockSpec((tm, LANES), tab), pl.BlockSpec((tm, LANES), tab),
                  full(wdq), full(qn), full(wuq), full(wkc), full(kvn), full(wkp)],
        out_specs=[pl.BlockSpec((tm, MLA_HEADS * HEAD_PAD), row),
                   pl.BlockSpec((tm, MLA_KV_LORA), row),
                   pl.BlockSpec((tm, MLA_ROPE), row)],
        out_shape=[jax.ShapeDtypeStruct((t, MLA_HEADS * HEAD_PAD), BF16),
                   jax.ShapeDtypeStruct((t, MLA_KV_LORA), F32),
                   jax.ShapeDtypeStruct((t, MLA_ROPE), F32)],
        compiler_params=_params("parallel"),
        name="mla_proj",
    )(x, cos, msin, sin, wdq, qn, wuq, wkc, kvn, wkp)


def _kv_expand_kernel(ckv_ref, kpe_ref, wkc_ref, wke_ref, wuv_ref, k_ref, v_ref):
    c = ckv_ref[...].astype(BF16)
    p = kpe_ref[...].astype(BF16)
    k_ref[...] = (_dot(c, wkc_ref[...]) + _dot(p, wke_ref[...])).astype(BF16)
    v_ref[...] = _dot(c, wuv_ref[...]).astype(BF16)


def _kv_expand(ckv, kpe, w, tm, n_rows):
    wkc, wke, wuv = w
    row = lambda i: (i, 0)
    fix = lambda i: (0, 0)
    full = lambda a: pl.BlockSpec(a.shape, fix)
    return pl.pallas_call(
        _kv_expand_kernel,
        grid=(n_rows // tm,),
        in_specs=[pl.BlockSpec((tm, MLA_KV_LORA), row), pl.BlockSpec((tm, MLA_ROPE), row),
                  full(wkc), full(wke), full(wuv)],
        out_specs=[pl.BlockSpec((tm, MLA_HEADS * HEAD_PAD), row),
                   pl.BlockSpec((tm, MLA_HEADS * MLA_V), row)],
        out_shape=[jax.ShapeDtypeStruct((n_rows, MLA_HEADS * HEAD_PAD), BF16),
                   jax.ShapeDtypeStruct((n_rows, MLA_HEADS * MLA_V), BF16)],
        compiler_params=_params("parallel"),
        name="mla_kv_expand",
    )(ckv, kpe, wkc, wke, wuv)


ATTN_HEADS = 4


def _attn_kernel(q_ref, k_ref, v_ref, o_ref, m_sc, l_sc, acc_sc, *, causal, tk):
    qi = pl.program_id(2)
    tq = q_ref.shape[0]
    m_sc[...] = jnp.full(m_sc.shape, -jnp.inf, F32)
    l_sc[...] = jnp.zeros(l_sc.shape, F32)
    acc_sc[...] = jnp.zeros(acc_sc.shape, F32)

    def block(ki, masked):
        rows = pl.ds(pl.multiple_of(ki * tk, tk), tk)
        vb = v_ref[rows, :]
        if masked:
            vis = (lax.broadcasted_iota(I32, (tq, tk), 1) // CHUNK) <= (lax.broadcasted_iota(I32, (tq, tk), 0) // CHUNK)
        for h in range(ATTN_HEADS):
            cols = slice(h * HEAD_PAD, (h + 1) * HEAD_PAD)
            s = _dot_nt(q_ref[:, cols], k_ref[rows, cols])
            if masked:
                s = jnp.where(vis, s, -jnp.inf)
            m_prev = m_sc[h]
            m_new = jnp.maximum(m_prev, jnp.max(s, -1, keepdims=True))
            a = jnp.exp2(m_prev - m_new)
            p = jnp.exp2(s - m_new[:, :1])
            l_sc[h] = a * l_sc[h] + jnp.sum(p, -1, keepdims=True)
            g = h // 2
            pv = _dot(p.astype(BF16), vb)[:, g * 2 * MLA_V:(g + 1) * 2 * MLA_V]
            acc_sc[h] = a * acc_sc[h] + pv
            m_sc[h] = m_new

    if causal:
        def body(ki, carry):
            block(ki, False)
            return carry
        lax.fori_loop(0, qi, body, 0)
        block(qi, True)
    else:
        block(0, False)

    lane = lax.broadcasted_iota(I32, (tq, 2 * MLA_V), 1)
    for g in range(ATTN_HEADS // 2):
        o = jnp.where(lane < MLA_V, acc_sc[2 * g] / l_sc[2 * g], acc_sc[2 * g + 1] / l_sc[2 * g + 1])
        o_ref[:, g * 2 * MLA_V:(g + 1) * 2 * MLA_V] = o.astype(BF16)


def _attention(q, k, v, *, nb, nq, tq, tk, kv_len, q_blk0, causal):
    hq = MLA_HEADS // ATTN_HEADS
    kvmap = lambda b, h, qi: (b, h)
    return pl.pallas_call(
        functools.partial(_attn_kernel, causal=causal, tk=tk),
        grid=(nb, hq, nq),
        in_specs=[pl.BlockSpec((tq, ATTN_HEADS * HEAD_PAD), lambda b, h, qi: (q_blk0 + b * nq + qi, h)),
                  pl.BlockSpec((kv_len, ATTN_HEADS * HEAD_PAD), kvmap),
                  pl.BlockSpec((kv_len, ATTN_HEADS * MLA_V), kvmap)],
        out_specs=pl.BlockSpec((tq, ATTN_HEADS * MLA_V), lambda b, h, qi: (b * nq + qi, h)),
        out_shape=jax.ShapeDtypeStruct((nb * nq * tq, MLA_HEADS * MLA_V), BF16),
        scratch_shapes=[pltpu.VMEM((ATTN_HEADS, tq, LANES), F32), pltpu.VMEM((ATTN_HEADS, tq, LANES), F32),
                        pltpu.VMEM((ATTN_HEADS, tq, 2 * MLA_V), F32)],
        compiler_params=_params("parallel", "parallel", "arbitrary"),
        name="mla_attention_causal" if causal else "mla_attention_full",
    )(q, k, v)


def _proj_ln_kernel(o_ref, x_ref, w_ref, g_ref, b_ref, y_ref, ya_ref, yb_ref):
    z = DEEPNORM_ALPHA * x_ref[...] + _dot(o_ref[...], w_ref[...])
    y = _layer_norm(z, g_ref[...], b_ref[...])
    y_ref[...] = y
    a, b = _pack_halves(y)
    ya_ref[...] = a
    yb_ref[...] = b


def _proj_ln(o, x, w, g, b, tm):
    t = x.shape[0]
    row = lambda i: (i, 0)
    fix = lambda i: (0, 0)
    return pl.pallas_call(
        _proj_ln_kernel,
        grid=(t // tm,),
        in_specs=[pl.BlockSpec((tm, D_MODEL), row), pl.BlockSpec((tm, D_MODEL), row),
                  pl.BlockSpec(w.shape, fix), pl.BlockSpec(g.shape, fix), pl.BlockSpec(b.shape, fix)],
        out_specs=[pl.BlockSpec((tm, D_MODEL), row), pl.BlockSpec((tm, HALF), row), pl.BlockSpec((tm, HALF), row)],
        out_shape=[jax.ShapeDtypeStruct((t, D_MODEL), F32), jax.ShapeDtypeStruct((t, HALF), I32),
                   jax.ShapeDtypeStruct((t, HALF), I32)],
        compiler_params=_params("parallel"),
        name="mixer_out_ln1",
    )(o, x, w, g, b)


def _gla_proj_kernel(x_ref, wq_ref, wk_ref, wv_ref, wg_ref, wa1_ref, wa2_ref, ba_ref,
                     q_ref, k_ref, v_ref, g_ref, la_ref):
    xb = x_ref[...].astype(BF16)
    q_ref[...] = (_dot(xb, wq_ref[...]) * (GLA_DK ** -0.5)).astype(BF16)
    k_ref[...] = _dot(xb, wk_ref[...]).astype(BF16)
    v_ref[...] = _dot(xb, wv_ref[...]).astype(BF16)
    gz = _dot(xb, wg_ref[...])
    g_ref[...] = (gz * _sigmoid(gz)).astype(BF16)
    z = _dot(_dot(xb, wa1_ref[...]).astype(BF16), wa2_ref[...]) + ba_ref[...]
    la_ref[...] = (jnp.minimum(z, 0.0) - jnp.log(1.0 + jnp.exp(-jnp.abs(z)))) * (1.0 / GLA_TAU)


def _gla_proj(x, w, tm):
    t = x.shape[0]
    row = lambda i: (i, 0)
    fix = lambda i: (0, 0)
    dk, dv = GLA_HEADS * GLA_DK, GLA_HEADS * GLA_DV
    return pl.pallas_call(
        _gla_proj_kernel,
        grid=(t // tm,),
        in_specs=[pl.BlockSpec((tm, D_MODEL), row)] + [pl.BlockSpec(a.shape, fix) for a in w],
        out_specs=[pl.BlockSpec((tm, dk), row), pl.BlockSpec((tm, dk), row), pl.BlockSpec((tm, dv), row),
                   pl.BlockSpec((tm, dv), row), pl.BlockSpec((tm, dk), row)],
        out_shape=[jax.ShapeDtypeStruct((t, dk), BF16), jax.ShapeDtypeStruct((t, dk), BF16),
                   jax.ShapeDtypeStruct((t, dv), BF16), jax.ShapeDtypeStruct((t, dv), BF16),
                   jax.ShapeDtypeStruct((t, dk), F32)],
        compiler_params=_params("parallel"),
        name="gla_proj",
    )(x, *w)


def _gla_chunk_kernel(q_ref, k_ref, v_ref, g_ref, la_ref, s0_ref, gn_ref, o_ref, st_ref, s_sc, *, nchunk):
    ti = pl.program_id(1)

    @pl.when(ti == 0)
    def _():
        s_sc[...] = s0_ref[0]

    c = CHUNK
    nsub = c // SUB
    r64 = lax.broadcasted_iota(I32, (c, c), 0)
    c64 = lax.broadcasted_iota(I32, (c, c), 1)
    tri = (r64 >= c64).astype(BF16)
    ones = jnp.ones((GLA_DK, LANES), BF16)
    lane16 = lax.broadcasted_iota(I32, (SUB, LANES), 1)
    row_l = lax.broadcasted_iota(I32, (c, LANES), 0)
    lane_l = lax.broadcasted_iota(I32, (c, LANES), 1)
    diag_vis = (lane_l // SUB == row_l // SUB) & (lane_l <= row_l)
    gn = gn_ref[...]

    def chunk(ci, carry):
        r0 = pl.multiple_of(ci * c, c)
        rows = pl.ds(r0, c)
        for h in range(GLA_HEADS):
            dk = slice(h * GLA_DK, (h + 1) * GLA_DK)
            dv = slice(h * GLA_DV, (h + 1) * GLA_DV)
            la = la_ref[rows, dk]
            la_hi = la.astype(BF16)
            la_lo = (la - la_hi.astype(F32)).astype(BF16)
            b = _dot(tri, la_hi) + _dot(tri, la_lo)
            q = q_ref[rows, dk].astype(F32)
            k = k_ref[rows, dk].astype(F32)
            v = v_ref[rows, dv]
            s_t = s_sc[h]
            o = _dot_nt((q * jnp.exp(b)).astype(BF16), s_t.astype(BF16))
            parts = []
            for s in range(c):
                i0 = (s // SUB) * SUB
                parts.append(q[i0:i0 + SUB] * (k[s:s + 1] * jnp.exp(b[i0:i0 + SUB] - b[s:s + 1])))
            rs = _dot(jnp.concatenate(parts, 0).astype(BF16), ones)
            blocks = []
            for i in range(nsub):
                dm = jnp.zeros((SUB, LANES), F32)
                for ss in range(SUB):
                    s = SUB * i + ss
                    dm = jnp.where(lane16 == s, rs[s * SUB:(s + 1) * SUB], dm)
                blocks.append(dm)
            att = jnp.where(diag_vis, jnp.concatenate(blocks, 0), 0.0)[:, :c]
            rk = jnp.concatenate([jnp.broadcast_to(b[SUB * j + SUB - 1:SUB * j + SUB], (SUB, GLA_DK))
                                  for j in range(nsub)], 0)
            kt = (k * jnp.exp(rk - b)).astype(BF16)
            for j in range(nsub - 1):
                rj = b[SUB * j + SUB - 1:SUB * j + SUB]
                qj = (q * jnp.exp(jnp.minimum(b - rj, 0.0))).astype(BF16)
                att = jnp.where((r64 >= SUB * (j + 1)) & (c64 // SUB == j), _dot_nt(qj, kt), att)
            o = o + _dot(att.astype(BF16), v)
            bend = b[c - 1:c]
            kh = (k * jnp.exp(bend - b)).astype(BF16)
            s_sc[h] = s_t * jnp.exp(bend) + _dot_tn(v, kh)
            on = _rms(o, gn)
            o_ref[rows, dv] = (on * g_ref[rows, dv].astype(F32)).astype(BF16)
        return carry

    lax.fori_loop(0, nchunk, chunk, 0)

    @pl.when(ti == pl.num_programs(1) - 1)
    def _():
        st_ref[0] = s_sc[...]


def _gla_chunks(q, k, v, g, la, s0t, gn, *, nb, nt, tb, blk0):
    dk, dv = GLA_HEADS * GLA_DK, GLA_HEADS * GLA_DV
    row = lambda b, t: (blk0 + b * nt + t, 0)
    st = lambda b, t: (b, 0, 0, 0)
    return pl.pallas_call(
        functools.partial(_gla_chunk_kernel, nchunk=tb // CHUNK),
        grid=(nb, nt),
        in_specs=[pl.BlockSpec((tb, dk), row), pl.BlockSpec((tb, dk), row), pl.BlockSpec((tb, dv), row),
                  pl.BlockSpec((tb, dv), row), pl.BlockSpec((tb, dk), row),
                  pl.BlockSpec((1, GLA_HEADS, GLA_DV, GLA_DK), st), pl.BlockSpec(gn.shape, lambda b, t: (0, 0))],
        out_specs=[pl.BlockSpec((tb, dv), lambda b, t: (b * nt + t, 0)),
                   pl.BlockSpec((1, GLA_HEADS, GLA_DV, GLA_DK), st)],
        out_shape=[jax.ShapeDtypeStruct((nb * nt * tb, dv), BF16),
                   jax.ShapeDtypeStruct((nb, GLA_HEADS, GLA_DV, GLA_DK), F32)],
        scratch_shapes=[pltpu.VMEM((GLA_HEADS, GLA_DV, GLA_DK), F32)],
        compiler_params=_params("parallel", "arbitrary"),
        name="gla_chunks",
    )(q, k, v, g, la, s0t, gn)


def _router_kernel(y_ref, wh_ref, wl_ref, bias_ref, u_ref, idx_ref, w_ref, rank_ref, cnt_ref, carry_sc):
    @pl.when(pl.program_id(0) == 0)
    def _():
        carry_sc[...] = jnp.zeros(carry_sc.shape, F32)

    y = y_ref[...]
    tm = y.shape[0]
    yh = y.astype(BF16)
    yl = (y - yh.astype(F32)).astype(BF16)
    wh, wl = wh_ref[...], wl_ref[...]
    logit = _dot_nt(wh, yh) + (_dot_nt(wh, yl) + _dot_nt(wl, yh))
    score = _sigmoid(logit)
    sel = score + bias_ref[...]
    shp = (N_GROUPS, GROUP_SIZE, tm)
    x3 = sel.reshape(shp)
    sc3 = score.reshape(shp)
    eidx = lax.broadcasted_iota(I32, shp, 1).astype(F32)
    gidx = lax.broadcasted_iota(I32, shp, 0).astype(F32)
    eflat = gidx * GROUP_SIZE + eidx
    neg = -jnp.inf
    m1 = jnp.max(x3, 1, keepdims=True)
    i1 = jnp.min(jnp.where(x3 == m1, eidx, float(GROUP_SIZE)), 1, keepdims=True)
    m2 = jnp.max(jnp.where(eidx == i1, neg, x3), 1, keepdims=True)
    gs = m1 + m2
    g1 = lax.broadcasted_iota(I32, (N_GROUPS, 1, tm), 0).astype(F32)
    gsel = jnp.zeros((N_GROUPS, 1, tm), F32)
    for _ in range(TOPK_GROUPS):
        m = jnp.max(gs, 0, keepdims=True)
        gi = jnp.min(jnp.where(gs == m, g1, float(N_GROUPS)), 0, keepdims=True)
        pick = g1 == gi
        gsel = jnp.where(pick, 1.0, gsel)
        gs = jnp.where(pick, neg, gs)
    masked = jnp.where(gsel > 0.0, x3, neg)
    chosen = jnp.zeros(shp, F32)
    idxs, ws = [], []
    for _ in range(TOP_K):
        m = jnp.max(jnp.max(masked, 0, keepdims=True), 1, keepdims=True)
        ei = jnp.min(jnp.min(jnp.where(masked == m, eflat, float(N_EXPERTS)), 0, keepdims=True), 1, keepdims=True)
        pick = eflat == ei
        ws.append(jnp.sum(jnp.sum(jnp.where(pick, sc3, 0.0), 0, keepdims=True), 1, keepdims=True))
        idxs.append(ei)
        masked = jnp.where(pick, neg, masked)
        chosen = jnp.where(pick, 1.0, chosen)
    wsum = ws[0]
    for wk in ws[1:]:
        wsum = wsum + wk
    m2d = chosen.reshape(N_EXPERTS, tm)
    before = _dot(m2d.astype(BF16), u_ref[...]) + carry_sc[:, :1]
    b3 = before.reshape(shp)
    for kk in range(TOP_K):
        rk = jnp.sum(jnp.sum(jnp.where(eflat == idxs[kk], b3, 0.0), 0, keepdims=True), 1, keepdims=True)
        idx_ref[kk:kk + 1, :] = idxs[kk].reshape(1, tm).astype(I32)
        rank_ref[kk:kk + 1, :] = rk.reshape(1, tm).astype(I32)
        w_ref[kk:kk + 1, :] = (ws[kk] / wsum * ROUTED_SCALE).reshape(1, tm)
    carry_sc[...] = carry_sc[...] + jnp.sum(m2d, -1, keepdims=True)
    cnt_ref[...] = carry_sc[...]


def _router(y, wh, wl, bias, upper, tm):
    t = y.shape[0]
    col = lambda i: (0, i)
    fix = lambda i: (0, 0)
    return pl.pallas_call(
        _router_kernel,
        grid=(t // tm,),
        in_specs=[pl.BlockSpec((tm, D_MODEL), lambda i: (i, 0)), pl.BlockSpec(wh.shape, fix),
                  pl.BlockSpec(wl.shape, fix), pl.BlockSpec(bias.shape, fix), pl.BlockSpec(upper.shape, fix)],
        out_specs=[pl.BlockSpec((TOP_K, tm), col), pl.BlockSpec((TOP_K, tm), col), pl.BlockSpec((TOP_K, tm), col),
                   pl.BlockSpec((N_EXPERTS, LANES), fix)],
        out_shape=[jax.ShapeDtypeStruct((TOP_K, t), I32), jax.ShapeDtypeStruct((TOP_K, t), F32),
                   jax.ShapeDtypeStruct((TOP_K, t), I32), jax.ShapeDtypeStruct((N_EXPERTS, LANES), F32)],
        scratch_shapes=[pltpu.VMEM((N_EXPERTS, LANES), F32)],
        compiler_params=_params("arbitrary"),
        name="moe_router",
    )(y, wh, wl, bias, upper)


def _sc_mesh():
    return plsc.VectorSubcoreMesh(core_axis_name="core", subcore_axis_name="subcore")


def _sc_scatter_rows(xa, xb, pos, n_out):
    t, c = xa.shape
    k = pos.shape[0]
    nblk = t // SC_WINDOW
    idx = pos.reshape(1, k * t)
    out = jax.ShapeDtypeStruct((n_out, c), xa.dtype)

    @functools.partial(pl.kernel, out_type=(out, out), mesh=_sc_mesh(), scratch_types=[])
    def scatter(xa_hbm, xb_hbm, i_hbm, oa_hbm, ob_hbm):
        for x_hbm, o_hbm in ((xa_hbm, oa_hbm), (xb_hbm, ob_hbm)):
            def body(x_vmem, i_vmem, o_hbm=o_hbm):
                pltpu.sync_copy(x_vmem, o_hbm.at[i_vmem.at[0]])

            pltpu.emit_pipeline(
                body,
                grid=(k * nblk,),
                in_specs=[pl.BlockSpec((SC_WINDOW, c), lambda i: (i % nblk, 0)),
                          pl.BlockSpec((1, SC_WINDOW), lambda i: (0, i))],
                out_specs=[],
                core_axis_name=("core", "subcore"),
                dimension_semantics=(pltpu.PARALLEL,),
                trace_scopes=False,
            )(x_hbm, i_hbm)

    return scatter(xa, xb, idx)


def _sc_gather_rows(ya, yb, pos):
    k, t = pos.shape
    c = ya.shape[1]
    n = k * t
    idx = pos.reshape(1, n)
    out = jax.ShapeDtypeStruct((n, c), ya.dtype)

    @functools.partial(pl.kernel, out_type=(out, out), mesh=_sc_mesh(), scratch_types=[])
    def gather(ya_hbm, yb_hbm, i_hbm, oa_hbm, ob_hbm):
        for y_hbm, o_hbm in ((ya_hbm, oa_hbm), (yb_hbm, ob_hbm)):
            def body(i_vmem, o_vmem, y_hbm=y_hbm):
                pltpu.sync_copy(y_hbm.at[i_vmem.at[0]], o_vmem)

            pltpu.emit_pipeline(
                body,
                grid=(n // SC_WINDOW,),
                in_specs=[pl.BlockSpec((1, SC_WINDOW), lambda i: (0, i))],
                out_specs=[pl.BlockSpec((SC_WINDOW, c), lambda i: (i, 0))],
                core_axis_name=("core", "subcore"),
                dimension_semantics=(pltpu.PARALLEL,),
                trace_scopes=False,
            )(i_hbm, o_hbm)

    return gather(ya, yb, idx)


def _expert_kernel(te_ref, nu_ref, xa_ref, xb_ref, wg_ref, wu_ref, wd_ref, ya_ref, yb_ref, wgu_sc, wd_sc):
    j = pl.program_id(0)

    @pl.when(j < nu_ref[0])
    def _():
        @pl.when((j == 0) | (te_ref[j] != te_ref[jnp.maximum(j - 1, 0)]))
        def _():
            wgu_sc[:, :EXPERT_FF] = wg_ref[0, 0].astype(BF16)
            wgu_sc[:, EXPERT_FF:] = wu_ref[0, 0].astype(BF16)
            wd_sc[...] = wd_ref[0, 0].astype(BF16)

        x = _unpack_halves(xa_ref[...], xb_ref[...]).astype(BF16)
        gu = _dot(x, wgu_sc[...])
        g, u = gu[:, :EXPERT_FF], gu[:, EXPERT_FF:]
        h = (g * _sigmoid(g) * u).astype(BF16)
        a, b = _pack_halves(_dot(h, wd_sc[...]))
        ya_ref[...] = a
        yb_ref[...] = b


def _experts(tile_expert, n_used, xa, xb, wg, wu, wd, layer, tr):
    n_rows = xa.shape[0]
    row = lambda j, te, nu: (jnp.minimum(j, nu[0] - 1), 0)
    wmap = lambda j, te, nu: (layer, te[jnp.minimum(j, nu[0] - 1)], 0, 0)
    grid_spec = pltpu.PrefetchScalarGridSpec(
        num_scalar_prefetch=2,
        grid=(n_rows // tr,),
        in_specs=[pl.BlockSpec((tr, HALF), row), pl.BlockSpec((tr, HALF), row),
                  pl.BlockSpec((1, 1, D_MODEL, EXPERT_FF), wmap), pl.BlockSpec((1, 1, D_MODEL, EXPERT_FF), wmap),
                  pl.BlockSpec((1, 1, EXPERT_FF, D_MODEL), wmap)],
        out_specs=[pl.BlockSpec((tr, HALF), row), pl.BlockSpec((tr, HALF), row)],
        scratch_shapes=[pltpu.VMEM((D_MODEL, 2 * EXPERT_FF), BF16), pltpu.VMEM((EXPERT_FF, D_MODEL), BF16)],
    )
    return pl.pallas_call(
        _expert_kernel,
        grid_spec=grid_spec,
        out_shape=[jax.ShapeDtypeStruct((n_rows, HALF), I32), jax.ShapeDtypeStruct((n_rows, HALF), I32)],
        compiler_params=_params("arbitrary"),
        name="moe_experts",
    )(tile_expert, n_used, xa, xb, wg, wu, wd)


def _combine_ln_kernel(x_ref, wt_ref, ya_ref, yb_ref, wsgu_ref, wsd_ref, g_ref, b_ref, y_ref):
    x = x_ref[...]
    gu = _dot(x.astype(BF16), wsgu_ref[...])
    g, u = gu[:, :EXPERT_FF], gu[:, EXPERT_FF:]
    acc = _dot((g * _sigmoid(g) * u).astype(BF16), wsd_ref[...])
    for k in range(TOP_K):
        acc = acc + wt_ref[:, k:k + 1] * _unpack_halves(ya_ref[k], yb_ref[k])
    y_ref[...] = _layer_norm(DEEPNORM_ALPHA * x + acc, g_ref[...], b_ref[...])


def _combine_ln(x, wt, yga, ygb, wsgu, wsd, g, b, tm):
    t = x.shape[0]
    row = lambda i: (i, 0)
    fix = lambda i: (0, 0)
    slot = lambda i: (0, i, 0)
    return pl.pallas_call(
        _combine_ln_kernel,
        grid=(t // tm,),
        in_specs=[pl.BlockSpec((tm, D_MODEL), row), pl.BlockSpec((tm, TOP_K), row),
                  pl.BlockSpec((TOP_K, tm, HALF), slot), pl.BlockSpec((TOP_K, tm, HALF), slot),
                  pl.BlockSpec(wsgu.shape, fix), pl.BlockSpec(wsd.shape, fix),
                  pl.BlockSpec(g.shape, fix), pl.BlockSpec(b.shape, fix)],
        out_specs=pl.BlockSpec((tm, D_MODEL), row),
        out_shape=jax.ShapeDtypeStruct((t, D_MODEL), F32),
        compiler_params=_params("parallel"),
        name="moe_combine_ln2",
    )(x, wt, yga.reshape(TOP_K, t, HALF), ygb.reshape(TOP_K, t, HALF), wsgu, wsd, g, b)


def _moe_layer(y, ya, yb, w, layer, tm, tr, upper):
    wrh, wrl, rbias, wg, wu, wd, wsgu, wsd, g2, b2 = w
    t = y.shape[0]
    idx, wts, rank, cnt = _router(y, wrh, wrl, rbias, upper, tm)
    counts = cnt[:, 0].astype(I32)
    ntile = (counts + (tr - 1)) // tr
    tile_end = jnp.cumsum(ntile)
    offs = (tile_end - ntile) * tr
    pos = rank
    for e in range(1, N_EXPERTS):
        pos = pos + jnp.where(idx == e, offs[e], 0)
    n_tiles = (t * TOP_K) // tr + N_EXPERTS
    tile_expert = jnp.minimum(jnp.sum((tile_end[None, :] <= jnp.arange(n_tiles, dtype=I32)[:, None]).astype(I32), 1),
                              N_EXPERTS - 1)
    n_used = tile_end[-1:].astype(I32)
    xsa, xsb = _sc_scatter_rows(ya, yb, pos, n_tiles * tr)
    ysa, ysb = _experts(tile_expert, n_used, xsa, xsb, wg, wu, wd, layer, tr)
    yga, ygb = _sc_gather_rows(ysa, ysb, pos)
    return _combine_ln(y, wts.T, yga, ygb, wsgu, wsd, g2, b2, min(tm, 256))


def _rope_tables(seq, past, dec_seq, n_dec_rows):
    half = MLA_ROPE // 2
    inv = ROPE_THETA ** (-jnp.arange(half, dtype=F32) / half)
    pos = jnp.concatenate([jnp.arange(seq), past + (jnp.arange(n_dec_rows) % dec_seq)]).astype(F32)
    ang = pos[:, None] * inv[None, :]
    cos, sin = jnp.cos(ang), jnp.sin(ang)
    n = pos.shape[0]
    z = lambda w: jnp.zeros((n, w), F32)
    cos_t = jnp.concatenate([cos, cos, jnp.ones((n, MLA_NOPE), F32), z(HEAD_PAD - MLA_ROPE - MLA_NOPE)], 1)
    msin_t = jnp.concatenate([-sin, z(LANES - half)], 1)
    sin_t = jnp.concatenate([z(half), sin, z(LANES - 2 * half)], 1)
    return cos_t, msin_t, sin_t


def _mla_weights(w_dq, q_norm, w_uq, w_dkv, kv_norm, w_uk, w_uv, w_o):
    wq = (w_uq * (MLA_SCALE * LOG2E)).reshape(MLA_Q_LORA, MLA_HEADS, MLA_NOPE + MLA_ROPE)
    wq = jnp.concatenate([wq[..., MLA_NOPE:], wq[..., :MLA_NOPE],
                          jnp.zeros((MLA_Q_LORA, MLA_HEADS, HEAD_PAD - MLA_NOPE - MLA_ROPE), F32)], -1)
    wq = wq.reshape(MLA_Q_LORA, MLA_HEADS * HEAD_PAD).astype(BF16)
    wkc_lat = w_dkv[:, :MLA_KV_LORA].astype(BF16)
    wkp = jnp.pad(w_dkv[:, MLA_KV_LORA:], ((0, 0), (0, LANES - MLA_ROPE))).astype(BF16)
    wk = w_uk.reshape(MLA_KV_LORA, MLA_HEADS, MLA_NOPE)
    wk = jnp.concatenate([jnp.zeros((MLA_KV_LORA, MLA_HEADS, MLA_ROPE), F32), wk,
                          jnp.zeros((MLA_KV_LORA, MLA_HEADS, HEAD_PAD - MLA_NOPE - MLA_ROPE), F32)], -1)
    wk = wk.reshape(MLA_KV_LORA, MLA_HEADS * HEAD_PAD).astype(BF16)
    eye = jnp.concatenate([jnp.eye(MLA_ROPE, dtype=F32), jnp.zeros((MLA_ROPE, HEAD_PAD - MLA_ROPE), F32)], 1)
    wke = jnp.tile(eye, (1, MLA_HEADS)).astype(BF16)
    proj = (w_dq.astype(BF16), q_norm.reshape(1, -1), wq, wkc_lat, kv_norm.reshape(1, -1), wkp)
    expand = (wk, wke, w_uv.astype(BF16))
    return proj, expand, w_o.astype(BF16)


def _gla_weights(w_q, w_k, w_v, w_a1, w_a2, b_a, w_g):
    wa1 = jnp.pad(w_a1, ((0, 0), (0, LANES - GLA_GATE_RANK))).astype(BF16)
    wa2 = jnp.pad(w_a2, ((0, LANES - GLA_GATE_RANK), (0, 0))).astype(BF16)
    return (w_q.astype(BF16), w_k.astype(BF16), w_v.astype(BF16), w_g.astype(BF16), wa1, wa2, b_a.reshape(1, -1))


def _moe_weights(w_router, router_bias, w_gate, w_up, w_down, ws_gate, ws_up, ws_down, g2, b2):
    wr_t = w_router.T
    wrh = wr_t.astype(BF16)
    wrl = (wr_t - wrh.astype(F32)).astype(BF16)
    wsgu = jnp.concatenate([ws_gate, ws_up], -1).astype(BF16)
    return (wrh, wrl, router_bias.reshape(-1, 1), w_gate, w_up, w_down, wsgu, ws_down.astype(BF16),
            g2.reshape(1, -1), b2.reshape(1, -1))


def kernel(x_prompt, x_sample, cache_ckv, cache_kpe, state_gla, mla_w_dq, mla_q_norm, mla_w_uq, mla_w_dkv, mla_kv_norm, mla_w_uk, mla_w_uv, mla_w_o, gla_w_q, gla_w_k, gla_w_v, gla_w_a1, gla_w_a2, gla_b_a, gla_w_g, gla_g_norm, gla_w_o, ln1_g, ln1_b, ln2_g, ln2_b, moe_w_router, moe_router_bias, moe_w_gate, moe_w_up, moe_w_down, moe_ws_gate, moe_ws_up, moe_ws_down):
    nb, seq, _ = x_prompt.shape
    ndb, dec_seq, _ = x_sample.shape
    past = cache_ckv.shape[2]
    assert dec_seq == CHUNK and (past + dec_seq) % 16 == 0

    mla_w = [_mla_weights(mla_w_dq[j], mla_q_norm[j], mla_w_uq[j], mla_w_dkv[j], mla_kv_norm[j],
                          mla_w_uk[j], mla_w_uv[j], mla_w_o[j]) for j in range(mla_w_dq.shape[0])]
    gla_w = [(_gla_weights(gla_w_q[j], gla_w_k[j], gla_w_v[j], gla_w_a1[j], gla_w_a2[j], gla_b_a[j], gla_w_g[j]),
              gla_w_o[j].astype(BF16), gla_g_norm[j].reshape(1, -1)) for j in range(gla_w_q.shape[0])]
    moe_w = [_moe_weights(moe_w_router[i], moe_router_bias[i], moe_w_gate, moe_w_up, moe_w_down,
                          moe_ws_gate[i], moe_ws_up[i], moe_ws_down[i], ln2_g[i], ln2_b[i]) for i in range(DEPTH)]
    ln1 = [(ln1_g[i].reshape(1, -1), ln1_b[i].reshape(1, -1)) for i in range(DEPTH)]

    def pipeline(xp, xs):
        nbp = xp.shape[0]
        tp = nbp * seq
        ts = 0 if xs is None else ndb * dec_seq
        t = tp + ts
        tm = 512 if (t % 512 == 0 and seq % 512 == 0 and ts <= 512) else 128
        tq = min(512, seq)
        tr = 512 if t >= 8192 else 128
        assert t % tm == 0 and tp % tm == 0 and ts % tm == 0 and tm % dec_seq == 0 and seq % tq == 0
        x = xp.reshape(tp, D_MODEL)
        if ts:
            x = jnp.concatenate([x, xs.reshape(ts, D_MODEL)], 0)
        tabs = _rope_tables(seq, past, dec_seq, max(ts, tm))
        upper = (jnp.arange(tm)[:, None] < jnp.arange(tm)[None, :]).astype(BF16)
        out = {k: [] for k in ("ckv_p", "kpe_p", "gla_p", "ckv_s", "kpe_s", "gla_s")}
        for i in range(DEPTH):
            j = i // 2
            if i % 2 == 0:
                proj_w, exp_w, wo = mla_w[j]
                q, ckv, kpe = _mla_proj(x, tabs, proj_w, tm, tp // tm, seq // tm)
                k_p, v_p = _kv_expand(ckv, kpe, exp_w, tm, tp)
                o = _attention(q, k_p, v_p, nb=nbp, nq=seq // tq, tq=tq, tk=tq, kv_len=seq, q_blk0=0, causal=True)
                out["ckv_p"].append(ckv[:tp].reshape(nbp, seq, -1))
                out["kpe_p"].append(kpe[:tp].reshape(nbp, seq, -1))
                if ts:
                    ckv_s, kpe_s = ckv[tp:].reshape(ndb, dec_seq, -1), kpe[tp:].reshape(ndb, dec_seq, -1)
                    c_all = jnp.concatenate([cache_ckv[j], ckv_s], 1).reshape(-1, MLA_KV_LORA)
                    p_all = jnp.concatenate([cache_kpe[j], kpe_s], 1).reshape(-1, MLA_ROPE)
                    n_all = c_all.shape[0]
                    k_s, v_s = _kv_expand(c_all, p_all, exp_w, 512 if n_all % 512 == 0 else past + dec_seq, n_all)
                    o_s = _attention(q, k_s, v_s, nb=ndb, nq=1, tq=dec_seq, tk=past + dec_seq,
                                     kv_len=past + dec_seq, q_blk0=tp // dec_seq, causal=False)
                    o = jnp.concatenate([o, o_s], 0)
                    out["ckv_s"].append(ckv_s)
                    out["kpe_s"].append(kpe_s)
            else:
                gw, wo, gn = gla_w[j]
                q, k, v, g, la = _gla_proj(x, gw, tm)
                zeros = jnp.zeros((nbp, GLA_HEADS, GLA_DV, GLA_DK), F32)
                o, st_p = _gla_chunks(q, k, v, g, la, zeros, gn, nb=nbp, nt=seq // tq, tb=tq, blk0=0)
                out["gla_p"].append(jnp.swapaxes(st_p, -1, -2))
                if ts:
                    s0 = jnp.swapaxes(state_gla[j], -1, -2)
                    o_s, st_s = _gla_chunks(q, k, v, g, la, s0, gn, nb=ndb, nt=1, tb=dec_seq, blk0=tp // dec_seq)
                    o = jnp.concatenate([o, o_s], 0)
                    out["gla_s"].append(jnp.swapaxes(st_s, -1, -2))
            y1, ya, yb = _proj_ln(o, x, wo, ln1[i][0], ln1[i][1], tm)
            x = _moe_layer(y1, ya, yb, moe_w[i], i, tm, tr, upper)
        out["y_p"] = x[:tp].reshape(nbp, seq, D_MODEL)
        if ts:
            out["y_s"] = x[tp:].reshape(ndb, dec_seq, D_MODEL)
        return out

    if nb % 2 == 0 and nb >= 2:
        h = nb // 2
        a, b = pipeline(x_prompt[:h], x_sample), pipeline(x_prompt[h:], None)
        cat = lambda key: jnp.concatenate([jnp.stack(a[key]), jnp.stack(b[key])], 1)
        y_p = jnp.concatenate([a["y_p"], b["y_p"]], 0)
        ckv_p, kpe_p, gla_p = cat("ckv_p"), cat("kpe_p"), cat("gla_p")
    else:
        a = pipeline(x_prompt, x_sample)
        y_p = a["y_p"]
        ckv_p, kpe_p, gla_p = jnp.stack(a["ckv_p"]), jnp.stack(a["kpe_p"]), jnp.stack(a["gla_p"])
    return (y_p, a["y_s"], ckv_p, kpe_p, gla_p,
            jnp.stack(a["ckv_s"]), jnp.stack(a["kpe_s"]), jnp.stack(a["gla_s"]))
```

```python
import functools

import jax
import jax.numpy as jnp
from jax import lax
from jax.experimental import pallas as pl
from jax.experimental.pallas import tpu as pltpu
from jax.experimental.pallas import tpu_sc as plsc

F32, BF16, I32 = jnp.float32, jnp.bfloat16, jnp.int32

D_MODEL = 1024
DEPTH = 4
CHUNK = 64
MLA_HEADS = 16
MLA_NOPE = 64
MLA_ROPE = 32
MLA_V = 64
MLA_Q_LORA = 384
MLA_KV_LORA = 256
MLA_SCALE = (MLA_NOPE + MLA_ROPE) ** -0.5
LOG2E = 1.4426950408889634
ROPE_THETA = 10000.0
GLA_HEADS = 4
GLA_DK = 128
GLA_DV = 256
GLA_GATE_RANK = 16
GLA_TAU = 16.0
N_EXPERTS = 64
TOP_K = 8
N_GROUPS = 8
GROUP_SIZE = N_EXPERTS // N_GROUPS
TOPK_GROUPS = 4
EXPERT_FF = 256
ROUTED_SCALE = 2.5
DEEPNORM_ALPHA = (2.0 * DEPTH) ** 0.25
NORM_EPS = 1e-5

LANES = 128
HEAD_PAD = 128
SUB = 16
SC_WINDOW = 128
HALF = D_MODEL // 4
VMEM_LIMIT = 48 * 1024 * 1024


def _dot(a, b):
    return jnp.dot(a, b, preferred_element_type=F32)


def _dot_nt(a, b):
    return lax.dot_general(a, b, (((1,), (1,)), ((), ())), preferred_element_type=F32)


def _dot_tn(a, b):
    return lax.dot_general(a, b, (((0,), (0,)), ((), ())), preferred_element_type=F32)


def _params(*sem):
    return pltpu.CompilerParams(dimension_semantics=sem, vmem_limit_bytes=VMEM_LIMIT)


def _sigmoid(x):
    return 1.0 / (1.0 + jnp.exp(-x))


def _rms(x, g):
    return x * lax.rsqrt(jnp.mean(x * x, -1, keepdims=True) + NORM_EPS) * g


def _layer_norm(z, g, b):
    mu = jnp.mean(z, -1, keepdims=True)
    zc = z - mu
    var = jnp.mean(zc * zc, -1, keepdims=True)
    return zc * lax.rsqrt(var + NORM_EPS) * g + b


def _rope128(x, cos, msin, sin):
    return x * cos + pltpu.roll(x, LANES - 16, 1) * msin + pltpu.roll(x, 16, 1) * sin


def _pack_halves(y):
    bits = lax.bitcast_convert_type(y.astype(BF16).astype(F32), I32)
    half = D_MODEL // 2
    p = (bits[:, :half] & jnp.int32(-65536)) | lax.shift_right_logical(bits[:, half:], jnp.int32(16))
    return p[:, :HALF], p[:, HALF:]


def _unpack_halves(a, b):
    p = jnp.concatenate([a, b], 1)
    hi = lax.bitcast_convert_type(p & jnp.int32(-65536), F32)
    lo = lax.bitcast_convert_type(lax.shift_left(p, jnp.int32(16)), F32)
    return jnp.concatenate([hi, lo], 1)


def _mla_proj_kernel(x_ref, cos_ref, msin_ref, sin_ref, wdq_ref, qn_ref, wuq_ref, wkc_ref, kvn_ref,
                     wkp_ref, q_ref, ckv_ref, kpe_ref):
    xb = x_ref[...].astype(BF16)
    cos, msin, sin = cos_ref[...], msin_ref[...], sin_ref[...]
    cq = _rms(_dot(xb, wdq_ref[...]), qn_ref[...]).astype(BF16)
    for hp in range(MLA_HEADS // 2):
        q2 = _dot(cq, wuq_ref[:, hp * 2 * HEAD_PAD:(hp + 1) * 2 * HEAD_PAD])
        for s in range(2):
            c0 = (2 * hp + s) * HEAD_PAD
            q_ref[:, c0:c0 + HEAD_PAD] = _rope128(q2[:, s * HEAD_PAD:(s + 1) * HEAD_PAD], cos, msin, sin).astype(BF16)
    ckv_ref[...] = _rms(_dot(xb, wkc_ref[...]), kvn_ref[...])
    kp = _rope128(_dot(xb, wkp_ref[...]), cos, msin, sin)
    kpe_ref[...] = kp[:, :MLA_ROPE]


def _mla_proj(x, tabs, w, tm, n_prompt_tiles, tiles_per_seq):
    t = x.shape[0]
    cos, msin, sin = tabs
    wdq, qn, wuq, wkc, kvn, wkp = w
    row = lambda i: (i, 0)
    fix = lambda i: (0, 0)
    tab = lambda i: (jnp.where(i < n_prompt_tiles, i % tiles_per_seq, tiles_per_seq), 0)
    full = lambda a: pl.BlockSpec(a.shape, fix)
    return pl.pallas_call(
        _mla_proj_kernel,
        grid=(t // tm,),
        in_specs=[pl.BlockSpec((tm, D_MODEL), row),
                  pl.BlockSpec((tm, LANES), tab), pl.BlockSpec((tm, LANES), tab), pl.BlockSpec((tm, LANES), tab),
                  full(wdq), full(qn), full(wuq), full(wkc), full(kvn), full(wkp)],
        out_specs=[pl.BlockSpec((tm, MLA_HEADS * HEAD_PAD), row),
                   pl.BlockSpec((tm, MLA_KV_LORA), row),
                   pl.BlockSpec((tm, MLA_ROPE), row)],
        out_shape=[jax.ShapeDtypeStruct((t, MLA_HEADS * HEAD_PAD), BF16),
                   jax.ShapeDtypeStruct((t, MLA_KV_LORA), F32),
                   jax.ShapeDtypeStruct((t, MLA_ROPE), F32)],
        compiler_params=_params("parallel"),
        name="mla_proj",
    )(x, cos, msin, sin, wdq, qn, wuq, wkc, kvn, wkp)


def _kv_expand_kernel(ckv_ref, kpe_ref, wkc_ref, wke_ref, wuv_ref, k_ref, v_ref, *, v_transposed):
    c = ckv_ref[...].astype(BF16)
    p = kpe_ref[...].astype(BF16)
    k_ref[...] = (_dot(c, wkc_ref[...]) + _dot(p, wke_ref[...])).astype(BF16)
    if v_transposed:
        v_ref[...] = _dot_nt(wuv_ref[...], c).astype(BF16)
    else:
        v_ref[...] = _dot(c, wuv_ref[...]).astype(BF16)


def _kv_expand(ckv, kpe, w, tm, n_rows, v_transposed):
    wkc, wke, wuv, wuv_t = w
    row = lambda i: (i, 0)
    fix = lambda i: (0, 0)
    full = lambda a: pl.BlockSpec(a.shape, fix)
    hv = MLA_HEADS * MLA_V
    if v_transposed:
        v_spec, v_shape, wv = pl.BlockSpec((hv, tm), lambda i: (0, i)), (hv, n_rows), wuv_t
    else:
        v_spec, v_shape, wv = pl.BlockSpec((tm, hv), row), (n_rows, hv), wuv
    return pl.pallas_call(
        functools.partial(_kv_expand_kernel, v_transposed=v_transposed),
        grid=(n_rows // tm,),
        in_specs=[pl.BlockSpec((tm, MLA_KV_LORA), row), pl.BlockSpec((tm, MLA_ROPE), row),
                  full(wkc), full(wke), full(wv)],
        out_specs=[pl.BlockSpec((tm, MLA_HEADS * HEAD_PAD), row), v_spec],
        out_shape=[jax.ShapeDtypeStruct((n_rows, MLA_HEADS * HEAD_PAD), BF16), jax.ShapeDtypeStruct(v_shape, BF16)],
        compiler_params=_params("parallel"),
        name="mla_kv_expand",
    )(ckv, kpe, wkc, wke, wv)


ATTN_HEADS = 4


def _attn_prompt_kernel(q_ref, k_ref, vt_ref, o_ref, m_sc, l_sc, acc_sc, *, tk):
    qi = pl.program_id(2)
    tq = q_ref.shape[0]
    m_sc[...] = jnp.full(m_sc.shape, -jnp.inf, F32)
    l_sc[...] = jnp.zeros(l_sc.shape, F32)
    acc_sc[...] = jnp.zeros(acc_sc.shape, F32)

    def block(ki, masked):
        keys = pl.ds(pl.multiple_of(ki * tk, tk), tk)
        if masked:
            vis = (lax.broadcasted_iota(I32, (tk, tq), 0) // CHUNK) <= (lax.broadcasted_iota(I32, (tk, tq), 1) // CHUNK)
        for h in range(ATTN_HEADS):
            cols = slice(h * HEAD_PAD, (h + 1) * HEAD_PAD)
            s = _dot_nt(k_ref[keys, cols], q_ref[:, cols])
            if masked:
                s = jnp.where(vis, s, -jnp.inf)
            m_prev = m_sc[h]
            m_new = jnp.maximum(m_prev, jnp.max(s, 0, keepdims=True))
            a = jnp.exp2(m_prev - m_new)
            p = jnp.exp2(s - m_new)
            l_sc[h] = a * l_sc[h] + jnp.sum(p, 0, keepdims=True)
            pv = _dot(vt_ref[h * MLA_V:(h + 1) * MLA_V, keys], p.astype(BF16))
            acc_sc[h] = a * acc_sc[h] + pv
            m_sc[h] = m_new

    def body(ki, carry):
        block(ki, False)
        return carry

    lax.fori_loop(0, qi, body, 0)
    block(qi, True)
    for h in range(ATTN_HEADS):
        o_ref[h * MLA_V:(h + 1) * MLA_V, :] = (acc_sc[h] / l_sc[h]).astype(BF16)


def _attention_prompt(q, k, vt, *, nb, seq, tq):
    hq = MLA_HEADS // ATTN_HEADS
    nq = seq // tq
    return pl.pallas_call(
        functools.partial(_attn_prompt_kernel, tk=tq),
        grid=(nb, hq, nq),
        in_specs=[pl.BlockSpec((tq, ATTN_HEADS * HEAD_PAD), lambda b, h, qi: (b * nq + qi, h)),
                  pl.BlockSpec((seq, ATTN_HEADS * HEAD_PAD), lambda b, h, qi: (b, h)),
                  pl.BlockSpec((ATTN_HEADS * MLA_V, seq), lambda b, h, qi: (h, b))],
        out_specs=pl.BlockSpec((ATTN_HEADS * MLA_V, tq), lambda b, h, qi: (h, b * nq + qi)),
        out_shape=jax.ShapeDtypeStruct((MLA_HEADS * MLA_V, nb * seq), BF16),
        scratch_shapes=[pltpu.VMEM((ATTN_HEADS, 1, tq), F32), pltpu.VMEM((ATTN_HEADS, 1, tq), F32),
                        pltpu.VMEM((ATTN_HEADS, MLA_V, tq), F32)],
        compiler_params=_params("parallel", "parallel", "arbitrary"),
        name="mla_attention_causal",
    )(q, k, vt)


def _attn_sample_kernel(q_ref, k_ref, v_ref, o_ref):
    tq = q_ref.shape[0]
    v = v_ref[...]
    lane_head = lax.broadcasted_iota(I32, (tq, ATTN_HEADS * MLA_V), 1) // MLA_V
    o = jnp.zeros((tq, ATTN_HEADS * MLA_V), F32)
    for h in range(ATTN_HEADS):
        cols = slice(h * HEAD_PAD, (h + 1) * HEAD_PAD)
        s = _dot_nt(q_ref[:, cols], k_ref[:, cols])
        p = jnp.exp2(s - jnp.max(s, -1, keepdims=True))
        pv = _dot(p.astype(BF16), v) / jnp.sum(p, -1, keepdims=True)
        o = jnp.where(lane_head == h, pv, o)
    o_ref[...] = o.astype(BF16)


def _attention_sample(q, k, v, *, nb, tq, kv_len, q_blk0):
    hq = MLA_HEADS // ATTN_HEADS
    return pl.pallas_call(
        _attn_sample_kernel,
        grid=(nb, hq),
        in_specs=[pl.BlockSpec((tq, ATTN_HEADS * HEAD_PAD), lambda b, h: (q_blk0 + b, h)),
                  pl.BlockSpec((kv_len, ATTN_HEADS * HEAD_PAD), lambda b, h: (b, h)),
                  pl.BlockSpec((kv_len, ATTN_HEADS * MLA_V), lambda b, h: (b, h))],
        out_specs=pl.BlockSpec((tq, ATTN_HEADS * MLA_V), lambda b, h: (b, h)),
        out_shape=jax.ShapeDtypeStruct((nb * tq, MLA_HEADS * MLA_V), BF16),
        compiler_params=_params("parallel", "parallel"),
        name="mla_attention_full",
    )(q, k, v)


def _proj_ln_kernel(*refs, transposed, n_main, has_tail):
    if has_tail:
        o_ref, ot_ref, x_ref, w_ref, g_ref, b_ref, y_ref, ya_ref, yb_ref = refs
    else:
        o_ref, x_ref, w_ref, g_ref, b_ref, y_ref, ya_ref, yb_ref = refs

    def finish(h):
        y = _layer_norm(DEEPNORM_ALPHA * x_ref[...] + h, g_ref[...], b_ref[...])
        y_ref[...] = y
        a, b = _pack_halves(y)
        ya_ref[...] = a
        yb_ref[...] = b

    def main():
        finish(_dot_tn(o_ref[...], w_ref[...]) if transposed else _dot(o_ref[...], w_ref[...]))

    if has_tail:
        pl.when(pl.program_id(0) < n_main)(main)
        pl.when(pl.program_id(0) >= n_main)(lambda: finish(_dot(ot_ref[...], w_ref[...])))
    else:
        main()


def _proj_ln(o, o_tail, x, w, g, b, tm, transposed):
    t = x.shape[0]
    n_main = (o.shape[1] if transposed else o.shape[0]) // tm
    has_tail = o_tail is not None
    row = lambda i: (i, 0)
    fix = lambda i: (0, 0)
    if transposed:
        o_spec = pl.BlockSpec((D_MODEL, tm), lambda i: (0, jnp.minimum(i, n_main - 1)))
    else:
        o_spec = pl.BlockSpec((tm, D_MODEL), lambda i: (jnp.minimum(i, n_main - 1), 0))
    tail = [o_tail] if has_tail else []
    tail_spec = [pl.BlockSpec((tm, D_MODEL), lambda i: (jnp.maximum(i - n_main, 0), 0))] if has_tail else []
    return pl.pallas_call(
        functools.partial(_proj_ln_kernel, transposed=transposed, n_main=n_main, has_tail=has_tail),
        grid=(t // tm,),
        in_specs=[o_spec] + tail_spec + [pl.BlockSpec((tm, D_MODEL), row),
                  pl.BlockSpec(w.shape, fix), pl.BlockSpec(g.shape, fix), pl.BlockSpec(b.shape, fix)],
        out_specs=[pl.BlockSpec((tm, D_MODEL), row), pl.BlockSpec((tm, HALF), row), pl.BlockSpec((tm, HALF), row)],
        out_shape=[jax.ShapeDtypeStruct((t, D_MODEL), F32), jax.ShapeDtypeStruct((t, HALF), I32),
                   jax.ShapeDtypeStruct((t, HALF), I32)],
        compiler_params=_params("parallel"),
        name="mixer_out_ln1",
    )(o, *tail, x, w, g, b)


def _gla_proj_kernel(x_ref, wq_ref, wk_ref, wv_ref, wg_ref, wa1_ref, wa2_ref, ba_ref,
                     q_ref, k_ref, v_ref, g_ref, la_ref):
    xb = x_ref[...].astype(BF16)
    q_ref[...] = (_dot(xb, wq_ref[...]) * (GLA_DK ** -0.5)).astype(BF16)
    k_ref[...] = _dot(xb, wk_ref[...]).astype(BF16)
    v_ref[...] = _dot(xb, wv_ref[...]).astype(BF16)
    gz = _dot(xb, wg_ref[...])
    g_ref[...] = (gz * _sigmoid(gz)).astype(BF16)
    z = _dot(_dot(xb, wa1_ref[...]).astype(BF16), wa2_ref[...]) + ba_ref[...]
    la_ref[...] = (jnp.minimum(z, 0.0) - jnp.log(1.0 + jnp.exp(-jnp.abs(z)))) * (1.0 / GLA_TAU)


def _gla_proj(x, w, tm):
    t = x.shape[0]
    row = lambda i: (i, 0)
    fix = lambda i: (0, 0)
    dk, dv = GLA_HEADS * GLA_DK, GLA_HEADS * GLA_DV
    return pl.pallas_call(
        _gla_proj_kernel,
        grid=(t // tm,),
        in_specs=[pl.BlockSpec((tm, D_MODEL), row)] + [pl.BlockSpec(a.shape, fix) for a in w],
        out_specs=[pl.BlockSpec((tm, dk), row), pl.BlockSpec((tm, dk), row), pl.BlockSpec((tm, dv), row),
                   pl.BlockSpec((tm, dv), row), pl.BlockSpec((tm, dk), row)],
        out_shape=[jax.ShapeDtypeStruct((t, dk), BF16), jax.ShapeDtypeStruct((t, dk), BF16),
                   jax.ShapeDtypeStruct((t, dv), BF16), jax.ShapeDtypeStruct((t, dv), BF16),
                   jax.ShapeDtypeStruct((t, dk), F32)],
        compiler_params=_params("parallel"),
        name="gla_proj",
    )(x, *w)


def _gla_chunk_kernel(q_ref, k_ref, v_ref, g_ref, la_ref, s0_ref, gn_ref, o_ref, st_ref, s_sc, *, nchunk):
    ti = pl.program_id(1)

    @pl.when(ti == 0)
    def _():
        s_sc[...] = s0_ref[0]

    c = CHUNK
    nsub = c // SUB
    r64 = lax.broadcasted_iota(I32, (c, c), 0)
    c64 = lax.broadcasted_iota(I32, (c, c), 1)
    tri = (r64 >= c64).astype(BF16)
    ones = jnp.ones((GLA_DK, LANES), BF16)
    lane16 = lax.broadcasted_iota(I32, (SUB, LANES), 1)
    row_l = lax.broadcasted_iota(I32, (c, LANES), 0)
    lane_l = lax.broadcasted_iota(I32, (c, LANES), 1)
    diag_vis = (lane_l // SUB == row_l // SUB) & (lane_l <= row_l)
    gn = gn_ref[...]

    def chunk(ci, carry):
        r0 = pl.multiple_of(ci * c, c)
        rows = pl.ds(r0, c)
        for h in range(GLA_HEADS):
            dk = slice(h * GLA_DK, (h + 1) * GLA_DK)
            dv = slice(h * GLA_DV, (h + 1) * GLA_DV)
            la = la_ref[rows, dk]
            la_hi = la.astype(BF16)
            la_lo = (la - la_hi.astype(F32)).astype(BF16)
            b = _dot(tri, la_hi) + _dot(tri, la_lo)
            q = q_ref[rows, dk].astype(F32)
            k = k_ref[rows, dk].astype(F32)
            v = v_ref[rows, dv]
            s_t = s_sc[h]
            o = _dot_nt((q * jnp.exp(b)).astype(BF16), s_t.astype(BF16))
            parts = []
            for s in range(c):
                i0 = (s // SUB) * SUB
                parts.append(q[i0:i0 + SUB] * (k[s:s + 1] * jnp.exp(b[i0:i0 + SUB] - b[s:s + 1])))
            rs = _dot(jnp.concatenate(parts, 0).astype(BF16), ones)
            blocks = []
            for i in range(nsub):
                dm = jnp.zeros((SUB, LANES), F32)
                for ss in range(SUB):
                    s = SUB * i + ss
                    dm = jnp.where(lane16 == s, rs[s * SUB:(s + 1) * SUB], dm)
                blocks.append(dm)
            att = jnp.where(diag_vis, jnp.concatenate(blocks, 0), 0.0)[:, :c]
            rk = jnp.concatenate([jnp.broadcast_to(b[SUB * j + SUB - 1:SUB * j + SUB], (SUB, GLA_DK))
                                  for j in range(nsub)], 0)
            kt = (k * jnp.exp(rk - b)).astype(BF16)
            for j in range(nsub - 1):
                rj = b[SUB * j + SUB - 1:SUB * j + SUB]
                qj = (q * jnp.exp(jnp.minimum(b - rj, 0.0))).astype(BF16)
                att = jnp.where((r64 >= SUB * (j + 1)) & (c64 // SUB == j), _dot_nt(qj, kt), att)
            o = o + _dot(att.astype(BF16), v)
            bend = b[c - 1:c]
            kh = (k * jnp.exp(bend - b)).astype(BF16)
            s_sc[h] = s_t * jnp.exp(bend) + _dot_tn(v, kh)
            on = _rms(o, gn)
            o_ref[rows, dv] = (on * g_ref[rows, dv].astype(F32)).astype(BF16)
        return carry

    lax.fori_loop(0, nchunk, chunk, 0)

    @pl.when(ti == pl.num_programs(1) - 1)
    def _():
        st_ref[0] = s_sc[...]


def _gla_chunks(q, k, v, g, la, s0t, gn, *, nb, nt, tb, blk0):
    dk, dv = GLA_HEADS * GLA_DK, GLA_HEADS * GLA_DV
    row = lambda b, t: (blk0 + b * nt + t, 0)
    st = lambda b, t: (b, 0, 0, 0)
    return pl.pallas_call(
        functools.partial(_gla_chunk_kernel, nchunk=tb // CHUNK),
        grid=(nb, nt),
        in_specs=[pl.BlockSpec((tb, dk), row), pl.BlockSpec((tb, dk), row), pl.BlockSpec((tb, dv), row),
                  pl.BlockSpec((tb, dv), row), pl.BlockSpec((tb, dk), row),
                  pl.BlockSpec((1, GLA_HEADS, GLA_DV, GLA_DK), st), pl.BlockSpec(gn.shape, lambda b, t: (0, 0))],
        out_specs=[pl.BlockSpec((tb, dv), lambda b, t: (b * nt + t, 0)),
                   pl.BlockSpec((1, GLA_HEADS, GLA_DV, GLA_DK), st)],
        out_shape=[jax.ShapeDtypeStruct((nb * nt * tb, dv), BF16),
                   jax.ShapeDtypeStruct((nb, GLA_HEADS, GLA_DV, GLA_DK), F32)],
        scratch_shapes=[pltpu.VMEM((GLA_HEADS, GLA_DV, GLA_DK), F32)],
        compiler_params=_params("parallel", "arbitrary"),
        name="gla_chunks",
    )(q, k, v, g, la, s0t, gn)


def _router_kernel(y_ref, wh_ref, wl_ref, bias_ref, u_ref, idx_ref, w_ref, rank_ref, cnt_ref, carry_sc):
    @pl.when(pl.program_id(0) == 0)
    def _():
        carry_sc[...] = jnp.zeros(carry_sc.shape, F32)

    y = y_ref[...]
    tm = y.shape[0]
    yh = y.astype(BF16)
    yl = (y - yh.astype(F32)).astype(BF16)
    wh, wl = wh_ref[...], wl_ref[...]
    logit = _dot_nt(wh, yh) + (_dot_nt(wh, yl) + _dot_nt(wl, yh))
    score = _sigmoid(logit)
    sel = score + bias_ref[...]
    shp = (N_GROUPS, GROUP_SIZE, tm)
    x3 = sel.reshape(shp)
    sc3 = score.reshape(shp)
    eidx = lax.broadcasted_iota(I32, shp, 1).astype(F32)
    gidx = lax.broadcasted_iota(I32, shp, 0).astype(F32)
    eflat = gidx * GROUP_SIZE + eidx
    neg = -jnp.inf
    m1 = jnp.max(x3, 1, keepdims=True)
    i1 = jnp.min(jnp.where(x3 == m1, eidx, float(GROUP_SIZE)), 1, keepdims=True)
    m2 = jnp.max(jnp.where(eidx == i1, neg, x3), 1, keepdims=True)
    gs = m1 + m2
    g1 = lax.broadcasted_iota(I32, (N_GROUPS, 1, tm), 0).astype(F32)
    gsel = jnp.zeros((N_GROUPS, 1, tm), F32)
    for _ in range(TOPK_GROUPS):
        m = jnp.max(gs, 0, keepdims=True)
        gi = jnp.min(jnp.where(gs == m, g1, float(N_GROUPS)), 0, keepdims=True)
        pick = g1 == gi
        gsel = jnp.where(pick, 1.0, gsel)
        gs = jnp.where(pick, neg, gs)
    masked = jnp.where(gsel > 0.0, x3, neg)
    chosen = jnp.zeros(shp, F32)
    idxs, ws = [], []
    for _ in range(TOP_K):
        m = jnp.max(jnp.max(masked, 0, keepdims=True), 1, keepdims=True)
        ei = jnp.min(jnp.min(jnp.where(masked == m, eflat, float(N_EXPERTS)), 0, keepdims=True), 1, keepdims=True)
        pick = eflat == ei
        ws.append(jnp.sum(jnp.sum(jnp.where(pick, sc3, 0.0), 0, keepdims=True), 1, keepdims=True))
        idxs.append(ei)
        masked = jnp.where(pick, neg, masked)
        chosen = jnp.where(pick, 1.0, chosen)
    wsum = ws[0]
    for wk in ws[1:]:
        wsum = wsum + wk
    m2d = chosen.reshape(N_EXPERTS, tm)
    before = _dot(m2d.astype(BF16), u_ref[...]) + carry_sc[:, :1]
    b3 = before.reshape(shp)
    for kk in range(TOP_K):
        rk = jnp.sum(jnp.sum(jnp.where(eflat == idxs[kk], b3, 0.0), 0, keepdims=True), 1, keepdims=True)
        idx_ref[kk:kk + 1, :] = idxs[kk].reshape(1, tm).astype(I32)
        rank_ref[kk:kk + 1, :] = rk.reshape(1, tm).astype(I32)
        w_ref[kk:kk + 1, :] = (ws[kk] / wsum * ROUTED_SCALE).reshape(1, tm)
    carry_sc[...] = carry_sc[...] + jnp.sum(m2d, -1, keepdims=True)
    cnt_ref[...] = carry_sc[...]


def _router(y, wh, wl, bias, upper, tm):
    t = y.shape[0]
    col = lambda i: (0, i)
    fix = lambda i: (0, 0)
    return pl.pallas_call(
        _router_kernel,
        grid=(t // tm,),
        in_specs=[pl.BlockSpec((tm, D_MODEL), lambda i: (i, 0)), pl.BlockSpec(wh.shape, fix),
                  pl.BlockSpec(wl.shape, fix), pl.BlockSpec(bias.shape, fix), pl.BlockSpec(upper.shape, fix)],
        out_specs=[pl.BlockSpec((TOP_K, tm), col), pl.BlockSpec((TOP_K, tm), col), pl.BlockSpec((TOP_K, tm), col),
                   pl.BlockSpec((N_EXPERTS, LANES), fix)],
        out_shape=[jax.ShapeDtypeStruct((TOP_K, t), I32), jax.ShapeDtypeStruct((TOP_K, t), F32),
                   jax.ShapeDtypeStruct((TOP_K, t), I32), jax.ShapeDtypeStruct((N_EXPERTS, LANES), F32)],
        scratch_shapes=[pltpu.VMEM((N_EXPERTS, LANES), F32)],
        compiler_params=_params("arbitrary"),
        name="moe_router",
    )(y, wh, wl, bias, upper)


def _sc_mesh():
    return plsc.VectorSubcoreMesh(core_axis_name="core", subcore_axis_name="subcore")


def _sc_scatter_rows(xa, xb, pos, n_out):
    t, c = xa.shape
    k = pos.shape[0]
    out = jax.ShapeDtypeStruct((n_out, c), xa.dtype)

    @functools.partial(pl.kernel, out_type=(out, out), mesh=_sc_mesh(), scratch_types=[])
    def scatter(xa_hbm, xb_hbm, i_hbm, oa_hbm, ob_hbm):
        for x_hbm, o_hbm in ((xa_hbm, oa_hbm), (xb_hbm, ob_hbm)):
            def body(x_vmem, i_vmem, o_hbm=o_hbm):
                for s in range(k):
                    pltpu.sync_copy(x_vmem, o_hbm.at[i_vmem.at[s]])

            pltpu.emit_pipeline(
                body,
                grid=(t // SC_WINDOW,),
                in_specs=[pl.BlockSpec((SC_WINDOW, c), lambda i: (i, 0)),
                          pl.BlockSpec((k, SC_WINDOW), lambda i: (0, i))],
                out_specs=[],
                core_axis_name=("core", "subcore"),
                dimension_semantics=(pltpu.PARALLEL,),
                trace_scopes=False,
            )(x_hbm, i_hbm)

    return scatter(xa, xb, pos)


def _sc_gather_rows(ya, yb, pos):
    k, t = pos.shape
    c = ya.shape[1]
    n = k * t
    idx = pos.reshape(1, n)
    out = jax.ShapeDtypeStruct((n, c), ya.dtype)

    @functools.partial(pl.kernel, out_type=(out, out), mesh=_sc_mesh(), scratch_types=[])
    def gather(ya_hbm, yb_hbm, i_hbm, oa_hbm, ob_hbm):
        for y_hbm, o_hbm in ((ya_hbm, oa_hbm), (yb_hbm, ob_hbm)):
            def body(i_vmem, o_vmem, y_hbm=y_hbm):
                pltpu.sync_copy(y_hbm.at[i_vmem.at[0]], o_vmem)

            pltpu.emit_pipeline(
                body,
                grid=(n // SC_WINDOW,),
                in_specs=[pl.BlockSpec((1, SC_WINDOW), lambda i: (0, i))],
                out_specs=[pl.BlockSpec((SC_WINDOW, c), lambda i: (i, 0))],
                core_axis_name=("core", "subcore"),
                dimension_semantics=(pltpu.PARALLEL,),
                trace_scopes=False,
            )(i_hbm, o_hbm)

    return gather(ya, yb, idx)


def _expert_kernel(te_ref, nu_ref, xa_ref, xb_ref, wg_ref, wu_ref, wd_ref, ya_ref, yb_ref, wgu_sc, wd_sc):
    j = pl.program_id(0)

    @pl.when(j < nu_ref[0])
    def _():
        @pl.when((j == 0) | (te_ref[j] != te_ref[jnp.maximum(j - 1, 0)]))
        def _():
            wgu_sc[:, :EXPERT_FF] = wg_ref[0, 0].astype(BF16)
            wgu_sc[:, EXPERT_FF:] = wu_ref[0, 0].astype(BF16)
            wd_sc[...] = wd_ref[0, 0].astype(BF16)

        x = _unpack_halves(xa_ref[...], xb_ref[...]).astype(BF16)
        gu = _dot(x, wgu_sc[...])
        g, u = gu[:, :EXPERT_FF], gu[:, EXPERT_FF:]
        h = (g * _sigmoid(g) * u).astype(BF16)
        a, b = _pack_halves(_dot(h, wd_sc[...]))
        ya_ref[...] = a
        yb_ref[...] = b


def _experts(tile_expert, n_used, xa, xb, wg, wu, wd, layer, tr):
    n_rows = xa.shape[0]
    row = lambda j, te, nu: (jnp.minimum(j, nu[0] - 1), 0)
    wmap = lambda j, te, nu: (layer, te[jnp.minimum(j, nu[0] - 1)], 0, 0)
    grid_spec = pltpu.PrefetchScalarGridSpec(
        num_scalar_prefetch=2,
        grid=(n_rows // tr,),
        in_specs=[pl.BlockSpec((tr, HALF), row), pl.BlockSpec((tr, HALF), row),
                  pl.BlockSpec((1, 1, D_MODEL, EXPERT_FF), wmap), pl.BlockSpec((1, 1, D_MODEL, EXPERT_FF), wmap),
                  pl.BlockSpec((1, 1, EXPERT_FF, D_MODEL), wmap)],
        out_specs=[pl.BlockSpec((tr, HALF), row), pl.BlockSpec((tr, HALF), row)],
        scratch_shapes=[pltpu.VMEM((D_MODEL, 2 * EXPERT_FF), BF16), pltpu.VMEM((EXPERT_FF, D_MODEL), BF16)],
    )
    return pl.pallas_call(
        _expert_kernel,
        grid_spec=grid_spec,
        out_shape=[jax.ShapeDtypeStruct((n_rows, HALF), I32), jax.ShapeDtypeStruct((n_rows, HALF), I32)],
        compiler_params=_params("arbitrary"),
        name="moe_experts",
    )(tile_expert, n_used, xa, xb, wg, wu, wd)


def _combine_ln_kernel(x_ref, wt_ref, ya_ref, yb_ref, wsgu_ref, wsd_ref, g_ref, b_ref, y_ref):
    x = x_ref[...]
    gu = _dot(x.astype(BF16), wsgu_ref[...])
    g, u = gu[:, :EXPERT_FF], gu[:, EXPERT_FF:]
    acc = _dot((g * _sigmoid(g) * u).astype(BF16), wsd_ref[...])
    for k in range(TOP_K):
        acc = acc + wt_ref[:, k:k + 1] * _unpack_halves(ya_ref[k], yb_ref[k])
    y_ref[...] = _layer_norm(DEEPNORM_ALPHA * x + acc, g_ref[...], b_ref[...])


def _combine_ln(x, wt, yga, ygb, wsgu, wsd, g, b, tm):
    t = x.shape[0]
    row = lambda i: (i, 0)
    fix = lambda i: (0, 0)
    slot = lambda i: (0, i, 0)
    return pl.pallas_call(
        _combine_ln_kernel,
        grid=(t // tm,),
        in_specs=[pl.BlockSpec((tm, D_MODEL), row), pl.BlockSpec((tm, TOP_K), row),
                  pl.BlockSpec((TOP_K, tm, HALF), slot), pl.BlockSpec((TOP_K, tm, HALF), slot),
                  pl.BlockSpec(wsgu.shape, fix), pl.BlockSpec(wsd.shape, fix),
                  pl.BlockSpec(g.shape, fix), pl.BlockSpec(b.shape, fix)],
        out_specs=pl.BlockSpec((tm, D_MODEL), row),
        out_shape=jax.ShapeDtypeStruct((t, D_MODEL), F32),
        compiler_params=_params("parallel"),
        name="moe_combine_ln2",
    )(x, wt, yga.reshape(TOP_K, t, HALF), ygb.reshape(TOP_K, t, HALF), wsgu, wsd, g, b)


def _moe_layer(y, ya, yb, w, layer, tm, tr, upper):
    wrh, wrl, rbias, wg, wu, wd, wsgu, wsd, g2, b2 = w
    t = y.shape[0]
    idx, wts, rank, cnt = _router(y, wrh, wrl, rbias, upper, tm)
    counts = cnt[:, 0].astype(I32)
    ntile = (counts + (tr - 1)) // tr
    tile_end = jnp.cumsum(ntile)
    offs = (tile_end - ntile) * tr
    pos = rank
    for e in range(1, N_EXPERTS):
        pos = pos + jnp.where(idx == e, offs[e], 0)
    n_tiles = (t * TOP_K) // tr + N_EXPERTS
    tile_expert = jnp.minimum(jnp.sum((tile_end[None, :] <= jnp.arange(n_tiles, dtype=I32)[:, None]).astype(I32), 1),
                              N_EXPERTS - 1)
    n_used = tile_end[-1:].astype(I32)
    xsa, xsb = _sc_scatter_rows(ya, yb, pos, n_tiles * tr)
    ysa, ysb = _experts(tile_expert, n_used, xsa, xsb, wg, wu, wd, layer, tr)
    yga, ygb = _sc_gather_rows(ysa, ysb, pos)
    return _combine_ln(y, wts.T, yga, ygb, wsgu, wsd, g2, b2, min(tm, 256))


def _rope_tables(seq, past, dec_seq, n_dec_rows):
    half = MLA_ROPE // 2
    inv = ROPE_THETA ** (-jnp.arange(half, dtype=F32) / half)
    pos = jnp.concatenate([jnp.arange(seq), past + (jnp.arange(n_dec_rows) % dec_seq)]).astype(F32)
    ang = pos[:, None] * inv[None, :]
    cos, sin = jnp.cos(ang), jnp.sin(ang)
    n = pos.shape[0]
    z = lambda w: jnp.zeros((n, w), F32)
    cos_t = jnp.concatenate([cos, cos, jnp.ones((n, MLA_NOPE), F32), z(HEAD_PAD - MLA_ROPE - MLA_NOPE)], 1)
    msin_t = jnp.concatenate([-sin, z(LANES - half)], 1)
    sin_t = jnp.concatenate([z(half), sin, z(LANES - 2 * half)], 1)
    return cos_t, msin_t, sin_t


def _mla_weights(w_dq, q_norm, w_uq, w_dkv, kv_norm, w_uk, w_uv, w_o):
    wq = (w_uq * (MLA_SCALE * LOG2E)).reshape(MLA_Q_LORA, MLA_HEADS, MLA_NOPE + MLA_ROPE)
    wq = jnp.concatenate([wq[..., MLA_NOPE:], wq[..., :MLA_NOPE],
                          jnp.zeros((MLA_Q_LORA, MLA_HEADS, HEAD_PAD - MLA_NOPE - MLA_ROPE), F32)], -1)
    wq = wq.reshape(MLA_Q_LORA, MLA_HEADS * HEAD_PAD).astype(BF16)
    wkc_lat = w_dkv[:, :MLA_KV_LORA].astype(BF16)
    wkp = jnp.pad(w_dkv[:, MLA_KV_LORA:], ((0, 0), (0, LANES - MLA_ROPE))).astype(BF16)
    wk = w_uk.reshape(MLA_KV_LORA, MLA_HEADS, MLA_NOPE)
    wk = jnp.concatenate([jnp.zeros((MLA_KV_LORA, MLA_HEADS, MLA_ROPE), F32), wk,
                          jnp.zeros((MLA_KV_LORA, MLA_HEADS, HEAD_PAD - MLA_NOPE - MLA_ROPE), F32)], -1)
    wk = wk.reshape(MLA_KV_LORA, MLA_HEADS * HEAD_PAD).astype(BF16)
    eye = jnp.concatenate([jnp.eye(MLA_ROPE, dtype=F32), jnp.zeros((MLA_ROPE, HEAD_PAD - MLA_ROPE), F32)], 1)
    wke = jnp.tile(eye, (1, MLA_HEADS)).astype(BF16)
    proj = (w_dq.astype(BF16), q_norm.reshape(1, -1), wq, wkc_lat, kv_norm.reshape(1, -1), wkp)
    expand = (wk, wke, w_uv.astype(BF16), w_uv.T.astype(BF16))
    return proj, expand, w_o.astype(BF16)


def _gla_weights(w_q, w_k, w_v, w_a1, w_a2, b_a, w_g):
    wa1 = jnp.pad(w_a1, ((0, 0), (0, LANES - GLA_GATE_RANK))).astype(BF16)
    wa2 = jnp.pad(w_a2, ((0, LANES - GLA_GATE_RANK), (0, 0))).astype(BF16)
    return (w_q.astype(BF16), w_k.astype(BF16), w_v.astype(BF16), w_g.astype(BF16), wa1, wa2, b_a.reshape(1, -1))


def _moe_weights(w_router, router_bias, w_gate, w_up, w_down, ws_gate, ws_up, ws_down, g2, b2):
    wr_t = w_router.T
    wrh = wr_t.astype(BF16)
    wrl = (wr_t - wrh.astype(F32)).astype(BF16)
    wsgu = jnp.concatenate([ws_gate, ws_up], -1).astype(BF16)
    return (wrh, wrl, router_bias.reshape(-1, 1), w_gate, w_up, w_down, wsgu, ws_down.astype(BF16),
            g2.reshape(1, -1), b2.reshape(1, -1))


def kernel(x_prompt, x_sample, cache_ckv, cache_kpe, state_gla, mla_w_dq, mla_q_norm, mla_w_uq, mla_w_dkv, mla_kv_norm, mla_w_uk, mla_w_uv, mla_w_o, gla_w_q, gla_w_k, gla_w_v, gla_w_a1, gla_w_a2, gla_b_a, gla_w_g, gla_g_norm, gla_w_o, ln1_g, ln1_b, ln2_g, ln2_b, moe_w_router, moe_router_bias, moe_w_gate, moe_w_up, moe_w_down, moe_ws_gate, moe_ws_up, moe_ws_down):
    nb, seq, _ = x_prompt.shape
    ndb, dec_seq, _ = x_sample.shape
    past = cache_ckv.shape[2]
    assert dec_seq == CHUNK and (past + dec_seq) % 16 == 0

    mla_w = [_mla_weights(mla_w_dq[j], mla_q_norm[j], mla_w_uq[j], mla_w_dkv[j], mla_kv_norm[j],
                          mla_w_uk[j], mla_w_uv[j], mla_w_o[j]) for j in range(mla_w_dq.shape[0])]
    gla_w = [(_gla_weights(gla_w_q[j], gla_w_k[j], gla_w_v[j], gla_w_a1[j], gla_w_a2[j], gla_b_a[j], gla_w_g[j]),
              gla_w_o[j].astype(BF16), gla_g_norm[j].reshape(1, -1)) for j in range(gla_w_q.shape[0])]
    moe_w = [_moe_weights(moe_w_router[i], moe_router_bias[i], moe_w_gate, moe_w_up, moe_w_down,
                          moe_ws_gate[i], moe_ws_up[i], moe_ws_down[i], ln2_g[i], ln2_b[i]) for i in range(DEPTH)]
    ln1 = [(ln1_g[i].reshape(1, -1), ln1_b[i].reshape(1, -1)) for i in range(DEPTH)]

    def pipeline(xp, xs):
        nbp = xp.shape[0]
        tp = nbp * seq
        ts = 0 if xs is None else ndb * dec_seq
        t = tp + ts
        tm = 512 if (t % 512 == 0 and seq % 512 == 0 and ts <= 512) else 128
        tq = min(512, seq)
        tr = 512 if t >= 8192 else 128
        assert t % tm == 0 and tp % tm == 0 and ts % tm == 0 and tm % dec_seq == 0 and seq % tq == 0
        x = xp.reshape(tp, D_MODEL)
        if ts:
            x = jnp.concatenate([x, xs.reshape(ts, D_MODEL)], 0)
        tabs = _rope_tables(seq, past, dec_seq, max(ts, tm))
        upper = (jnp.arange(tm)[:, None] < jnp.arange(tm)[None, :]).astype(BF16)
        out = {k: [] for k in ("ckv_p", "kpe_p", "gla_p", "ckv_s", "kpe_s", "gla_s")}
        for i in range(DEPTH):
            j = i // 2
            if i % 2 == 0:
                proj_w, exp_w, wo = mla_w[j]
                q, ckv, kpe = _mla_proj(x, tabs, proj_w, tm, tp // tm, seq // tm)
                k_p, vt_p = _kv_expand(ckv, kpe, exp_w, tm, tp, True)
                o = _attention_prompt(q, k_p, vt_p, nb=nbp, seq=seq, tq=tq)
                o_s = None
                out["ckv_p"].append(ckv[:tp].reshape(nbp, seq, -1))
                out["kpe_p"].append(kpe[:tp].reshape(nbp, seq, -1))
                if ts:
                    ckv_s, kpe_s = ckv[tp:].reshape(ndb, dec_seq, -1), kpe[tp:].reshape(ndb, dec_seq, -1)
                    c_all = jnp.concatenate([cache_ckv[j], ckv_s], 1).reshape(-1, MLA_KV_LORA)
                    p_all = jnp.concatenate([cache_kpe[j], kpe_s], 1).reshape(-1, MLA_ROPE)
                    n_all = c_all.shape[0]
                    k_s, v_s = _kv_expand(c_all, p_all, exp_w, 512 if n_all % 512 == 0 else past + dec_seq, n_all,
                                          False)
                    o_s = _attention_sample(q, k_s, v_s, nb=ndb, tq=dec_seq, kv_len=past + dec_seq,
                                            q_blk0=tp // dec_seq)
                    out["ckv_s"].append(ckv_s)
                    out["kpe_s"].append(kpe_s)
            else:
                gw, wo, gn = gla_w[j]
                q, k, v, g, la = _gla_proj(x, gw, tm)
                zeros = jnp.zeros((nbp, GLA_HEADS, GLA_DV, GLA_DK), F32)
                o, st_p = _gla_chunks(q, k, v, g, la, zeros, gn, nb=nbp, nt=seq // tq, tb=tq, blk0=0)
                o_s = None
                out["gla_p"].append(jnp.swapaxes(st_p, -1, -2))
                if ts:
                    s0 = jnp.swapaxes(state_gla[j], -1, -2)
                    o_s, st_s = _gla_chunks(q, k, v, g, la, s0, gn, nb=ndb, nt=1, tb=dec_seq, blk0=tp // dec_seq)
                    out["gla_s"].append(jnp.swapaxes(st_s, -1, -2))
            y1, ya, yb = _proj_ln(o, o_s, x, wo, ln1[i][0], ln1[i][1], tm, transposed=(i % 2 == 0))
            x = _moe_layer(y1, ya, yb, moe_w[i], i, tm, tr, upper)
        out["y_p"] = x[:tp].reshape(nbp, seq, D_MODEL)
        if ts:
            out["y_s"] = x[tp:].reshape(ndb, dec_seq, D_MODEL)
        return out

    if nb % 2 == 0 and nb >= 2:
        h = nb // 2
        a, b = pipeline(x_prompt[:h], x_sample), pipeline(x_prompt[h:], None)
        cat = lambda key: jnp.concatenate([jnp.stack(a[key]), jnp.stack(b[key])], 1)
        y_p = jnp.concatenate([a["y_p"], b["y_p"]], 0)
        ckv_p, kpe_p, gla_p = cat("ckv_p"), cat("kpe_p"), cat("gla_p")
    else:
        a = pipeline(x_prompt, x_sample)
        y_p = a["y_p"]
        ckv_p, kpe_p, gla_p = jnp.stack(a["ckv_p"]), jnp.stack(a["kpe_p"]), jnp.stack(a["gla_p"])
    return (y_p, a["y_s"], ckv_p, kpe_p, gla_p,
            jnp.stack(a["ckv_s"]), jnp.stack(a["kpe_s"]), jnp.stack(a["gla_s"]))
```

```python
import functools

import jax
import jax.numpy as jnp
from jax import lax
from jax.experimental import pallas as pl
from jax.experimental.pallas import tpu as pltpu
from jax.experimental.pallas import tpu_sc as plsc

F32, BF16, I32 = jnp.float32, jnp.bfloat16, jnp.int32

D_MODEL = 1024
DEPTH = 4
CHUNK = 64
MLA_HEADS = 16
MLA_NOPE = 64
MLA_ROPE = 32
MLA_V = 64
MLA_Q_LORA = 384
MLA_KV_LORA = 256
MLA_SCALE = (MLA_NOPE + MLA_ROPE) ** -0.5
LOG2E = 1.4426950408889634
ROPE_THETA = 10000.0
GLA_HEADS = 4
GLA_DK = 128
GLA_DV = 256
GLA_GATE_RANK = 16
GLA_TAU = 16.0
N_EXPERTS = 64
TOP_K = 8
N_GROUPS = 8
GROUP_SIZE = N_EXPERTS // N_GROUPS
TOPK_GROUPS = 4
EXPERT_FF = 256
ROUTED_SCALE = 2.5
DEEPNORM_ALPHA = (2.0 * DEPTH) ** 0.25
NORM_EPS = 1e-5

LANES = 128
HEAD_PAD = 128
SUB = 16
SC_WINDOW = 128
HALF = D_MODEL // 4
VMEM_LIMIT = 48 * 1024 * 1024


def _dot(a, b):
    return jnp.dot(a, b, preferred_element_type=F32)


def _dot_nt(a, b):
    return lax.dot_general(a, b, (((1,), (1,)), ((), ())), preferred_element_type=F32)


def _dot_tn(a, b):
    return lax.dot_general(a, b, (((0,), (0,)), ((), ())), preferred_element_type=F32)


def _params(*sem):
    return pltpu.CompilerParams(dimension_semantics=sem, vmem_limit_bytes=VMEM_LIMIT)


def _sigmoid(x):
    return 1.0 / (1.0 + jnp.exp(-x))


def _rms(x, g):
    return x * lax.rsqrt(jnp.mean(x * x, -1, keepdims=True) + NORM_EPS) * g


def _layer_norm(z, g, b):
    mu = jnp.mean(z, -1, keepdims=True)
    zc = z - mu
    var = jnp.mean(zc * zc, -1, keepdims=True)
    return zc * lax.rsqrt(var + NORM_EPS) * g + b


def _rope128(x, cos, msin, sin):
    return x * cos + pltpu.roll(x, LANES - 16, 1) * msin + pltpu.roll(x, 16, 1) * sin


def _pack_halves(y):
    bits = lax.bitcast_convert_type(y.astype(BF16).astype(F32), I32)
    half = D_MODEL // 2
    p = (bits[:, :half] & jnp.int32(-65536)) | lax.shift_right_logical(bits[:, half:], jnp.int32(16))
    return p[:, :HALF], p[:, HALF:]


def _unpack_halves(a, b):
    p = jnp.concatenate([a, b], 1)
    hi = lax.bitcast_convert_type(p & jnp.int32(-65536), F32)
    lo = lax.bitcast_convert_type(lax.shift_left(p, jnp.int32(16)), F32)
    return jnp.concatenate([hi, lo], 1)


def _mla_proj_kernel(x_ref, cos_ref, msin_ref, sin_ref, wdq_ref, qn_ref, wuq_ref, wkc_ref, kvn_ref,
                     wkp_ref, q_ref, ckv_ref, kpe_ref):
    xb = x_ref[...].astype(BF16)
    cos, msin, sin = cos_ref[...], msin_ref[...], sin_ref[...]
    cq = _rms(_dot(xb, wdq_ref[...]), qn_ref[...]).astype(BF16)
    for hp in range(MLA_HEADS // 2):
        q2 = _dot(cq, wuq_ref[:, hp * 2 * HEAD_PAD:(hp + 1) * 2 * HEAD_PAD])
        for s in range(2):
            c0 = (2 * hp + s) * HEAD_PAD
            q_ref[:, c0:c0 + HEAD_PAD] = _rope128(q2[:, s * HEAD_PAD:(s + 1) * HEAD_PAD], cos, msin, sin).astype(BF16)
    ckv_ref[...] = _rms(_dot(xb, wkc_ref[...]), kvn_ref[...])
    kp = _rope128(_dot(xb, wkp_ref[...]), cos, msin, sin)
    kpe_ref[...] = kp[:, :MLA_ROPE]


def _mla_proj(x, tabs, w, tm, n_prompt_tiles, tiles_per_seq):
    t = x.shape[0]
    cos, msin, sin = tabs
    wdq, qn, wuq, wkc, kvn, wkp = w
    row = lambda i: (i, 0)
    fix = lambda i: (0, 0)
    tab = lambda i: (jnp.where(i < n_prompt_tiles, i % tiles_per_seq, tiles_per_seq), 0)
    full = lambda a: pl.BlockSpec(a.shape, fix)
    return pl.pallas_call(
        _mla_proj_kernel,
        grid=(t // tm,),
        in_specs=[pl.BlockSpec((tm, D_MODEL), row),
                  pl.BlockSpec((tm, LANES), tab), pl.BlockSpec((tm, LANES), tab), pl.BlockSpec((tm, LANES), tab),
                  full(wdq), full(qn), full(wuq), full(wkc), full(kvn), full(wkp)],
        out_specs=[pl.BlockSpec((tm, MLA_HEADS * HEAD_PAD), row),
                   pl.BlockSpec((tm, MLA_KV_LORA), row),
                   pl.BlockSpec((tm, MLA_ROPE), row)],
        out_shape=[jax.ShapeDtypeStruct((t, MLA_HEADS * HEAD_PAD), BF16),
                   jax.ShapeDtypeStruct((t, MLA_KV_LORA), F32),
                   jax.ShapeDtypeStruct((t, MLA_ROPE), F32)],
        compiler_params=_params("parallel"),
        name="mla_proj",
    )(x, cos, msin, sin, wdq, qn, wuq, wkc, kvn, wkp)


def _kv_expand_kernel(ckv_ref, kpe_ref, wkc_ref, wke_ref, wuv_ref, k_ref, v_ref, *, v_transposed):
    c = ckv_ref[...].astype(BF16)
    p = kpe_ref[...].astype(BF16)
    k_ref[...] = (_dot(c, wkc_ref[...]) + _dot(p, wke_ref[...])).astype(BF16)
    if v_transposed:
        v_ref[...] = _dot_nt(wuv_ref[...], c).astype(BF16)
    else:
        v_ref[...] = _dot(c, wuv_ref[...]).astype(BF16)


def _kv_expand(ckv, kpe, w, tm, n_rows, v_transposed):
    wkc, wke, wuv, wuv_t = w
    row = lambda i: (i, 0)
    fix = lambda i: (0, 0)
    full = lambda a: pl.BlockSpec(a.shape, fix)
    hv = MLA_HEADS * MLA_V
    if v_transposed:
        v_spec, v_shape, wv = pl.BlockSpec((hv, tm), lambda i: (0, i)), (hv, n_rows), wuv_t
    else:
        v_spec, v_shape, wv = pl.BlockSpec((tm, hv), row), (n_rows, hv), wuv
    return pl.pallas_call(
        functools.partial(_kv_expand_kernel, v_transposed=v_transposed),
        grid=(n_rows // tm,),
        in_specs=[pl.BlockSpec((tm, MLA_KV_LORA), row), pl.BlockSpec((tm, MLA_ROPE), row),
                  full(wkc), full(wke), full(wv)],
        out_specs=[pl.BlockSpec((tm, MLA_HEADS * HEAD_PAD), row), v_spec],
        out_shape=[jax.ShapeDtypeStruct((n_rows, MLA_HEADS * HEAD_PAD), BF16), jax.ShapeDtypeStruct(v_shape, BF16)],
        compiler_params=_params("parallel"),
        name="mla_kv_expand",
    )(ckv, kpe, wkc, wke, wv)


ATTN_HEADS = 4
ATTN_BLOCK = 1024
ATTN_KEY_BLOCK = 1024


def _attn_prompt_kernel(q_ref, k_ref, vt_ref, o_ref, m_sc, l_sc, acc_sc, *, tk):
    qi = pl.program_id(2)
    tq = q_ref.shape[0]
    m_sc[...] = jnp.full(m_sc.shape, -jnp.inf, F32)
    l_sc[...] = jnp.zeros(l_sc.shape, F32)
    acc_sc[...] = jnp.zeros(acc_sc.shape, F32)

    nsub = tq // tk

    def block(ki, diag):
        keys = pl.ds(pl.multiple_of(ki * tk, tk), tk)
        masked = diag is not None
        if masked:
            vis = ((lax.broadcasted_iota(I32, (tk, tq), 0) + diag * tk) // CHUNK) <= (lax.broadcasted_iota(I32, (tk, tq), 1) // CHUNK)
        for h in range(ATTN_HEADS):
            cols = slice(h * HEAD_PAD, (h + 1) * HEAD_PAD)
            s = _dot_nt(k_ref[keys, cols], q_ref[:, cols])
            if masked:
                s = jnp.where(vis, s, -jnp.inf)
            m_prev = m_sc[h]
            m_new = jnp.maximum(m_prev, jnp.max(s, 0, keepdims=True))
            a = jnp.exp2(m_prev - m_new)
            p = jnp.exp2(s - m_new)
            l_sc[h] = a * l_sc[h] + jnp.sum(p, 0, keepdims=True)
            pv = _dot(vt_ref[h * MLA_V:(h + 1) * MLA_V, keys], p.astype(BF16))
            acc_sc[h] = a * acc_sc[h] + pv
            m_sc[h] = m_new

    def body(ki, carry):
        block(ki, None)
        return carry

    lax.fori_loop(0, qi * nsub, body, 0)
    for j in range(nsub):
        block(qi * nsub + j, j)
    for h in range(ATTN_HEADS):
        o_ref[h * MLA_V:(h + 1) * MLA_V, :] = (acc_sc[h] / l_sc[h]).astype(BF16)


def _attention_prompt(q, k, vt, *, nb, seq, tq):
    hq = MLA_HEADS // ATTN_HEADS
    nq = seq // tq
    return pl.pallas_call(
        functools.partial(_attn_prompt_kernel, tk=min(ATTN_KEY_BLOCK, tq)),
        grid=(nb, hq, nq),
        in_specs=[pl.BlockSpec((tq, ATTN_HEADS * HEAD_PAD), lambda b, h, qi: (b * nq + qi, h)),
                  pl.BlockSpec((seq, ATTN_HEADS * HEAD_PAD), lambda b, h, qi: (b, h)),
                  pl.BlockSpec((ATTN_HEADS * MLA_V, seq), lambda b, h, qi: (h, b))],
        out_specs=pl.BlockSpec((ATTN_HEADS * MLA_V, tq), lambda b, h, qi: (h, b * nq + qi)),
        out_shape=jax.ShapeDtypeStruct((MLA_HEADS * MLA_V, nb * seq), BF16),
        scratch_shapes=[pltpu.VMEM((ATTN_HEADS, 1, tq), F32), pltpu.VMEM((ATTN_HEADS, 1, tq), F32),
                        pltpu.VMEM((ATTN_HEADS, MLA_V, tq), F32)],
        compiler_params=_params("parallel", "parallel", "arbitrary"),
        name="mla_attention_causal",
    )(q, k, vt)


def _attn_sample_kernel(q_ref, k_ref, v_ref, o_ref):
    tq = q_ref.shape[0]
    v = v_ref[...]
    lane_head = lax.broadcasted_iota(I32, (tq, ATTN_HEADS * MLA_V), 1) // MLA_V
    o = jnp.zeros((tq, ATTN_HEADS * MLA_V), F32)
    for h in range(ATTN_HEADS):
        cols = slice(h * HEAD_PAD, (h + 1) * HEAD_PAD)
        s = _dot_nt(q_ref[:, cols], k_ref[:, cols])
        p = jnp.exp2(s - jnp.max(s, -1, keepdims=True))
        pv = _dot(p.astype(BF16), v) / jnp.sum(p, -1, keepdims=True)
        o = jnp.where(lane_head == h, pv, o)
    o_ref[...] = o.astype(BF16)


def _attention_sample(q, k, v, *, nb, tq, kv_len, q_blk0):
    hq = MLA_HEADS // ATTN_HEADS
    return pl.pallas_call(
        _attn_sample_kernel,
        grid=(nb, hq),
        in_specs=[pl.BlockSpec((tq, ATTN_HEADS * HEAD_PAD), lambda b, h: (q_blk0 + b, h)),
                  pl.BlockSpec((kv_len, ATTN_HEADS * HEAD_PAD), lambda b, h: (b, h)),
                  pl.BlockSpec((kv_len, ATTN_HEADS * MLA_V), lambda b, h: (b, h))],
        out_specs=pl.BlockSpec((tq, ATTN_HEADS * MLA_V), lambda b, h: (b, h)),
        out_shape=jax.ShapeDtypeStruct((nb * tq, MLA_HEADS * MLA_V), BF16),
        compiler_params=_params("parallel", "parallel"),
        name="mla_attention_full",
    )(q, k, v)


def _proj_ln_kernel(*refs, transposed, n_main, has_tail):
    if has_tail:
        o_ref, ot_ref, x_ref, w_ref, g_ref, b_ref, y_ref, ya_ref, yb_ref = refs
    else:
        o_ref, x_ref, w_ref, g_ref, b_ref, y_ref, ya_ref, yb_ref = refs

    def finish(h):
        y = _layer_norm(DEEPNORM_ALPHA * x_ref[...] + h, g_ref[...], b_ref[...])
        y_ref[...] = y
        a, b = _pack_halves(y)
        ya_ref[...] = a
        yb_ref[...] = b

    def main():
        finish(_dot_tn(o_ref[...], w_ref[...]) if transposed else _dot(o_ref[...], w_ref[...]))

    if has_tail:
        pl.when(pl.program_id(0) < n_main)(main)
        pl.when(pl.program_id(0) >= n_main)(lambda: finish(_dot(ot_ref[...], w_ref[...])))
    else:
        main()


def _proj_ln(o, o_tail, x, w, g, b, tm, transposed):
    t = x.shape[0]
    n_main = (o.shape[1] if transposed else o.shape[0]) // tm
    has_tail = o_tail is not None
    row = lambda i: (i, 0)
    fix = lambda i: (0, 0)
    if transposed:
        o_spec = pl.BlockSpec((D_MODEL, tm), lambda i: (0, jnp.minimum(i, n_main - 1)))
    else:
        o_spec = pl.BlockSpec((tm, D_MODEL), lambda i: (jnp.minimum(i, n_main - 1), 0))
    tail = [o_tail] if has_tail else []
    tail_spec = [pl.BlockSpec((tm, D_MODEL), lambda i: (jnp.maximum(i - n_main, 0), 0))] if has_tail else []
    return pl.pallas_call(
        functools.partial(_proj_ln_kernel, transposed=transposed, n_main=n_main, has_tail=has_tail),
        grid=(t // tm,),
        in_specs=[o_spec] + tail_spec + [pl.BlockSpec((tm, D_MODEL), row),
                  pl.BlockSpec(w.shape, fix), pl.BlockSpec(g.shape, fix), pl.BlockSpec(b.shape, fix)],
        out_specs=[pl.BlockSpec((tm, D_MODEL), row), pl.BlockSpec((tm, HALF), row), pl.BlockSpec((tm, HALF), row)],
        out_shape=[jax.ShapeDtypeStruct((t, D_MODEL), F32), jax.ShapeDtypeStruct((t, HALF), I32),
                   jax.ShapeDtypeStruct((t, HALF), I32)],
        compiler_params=_params("parallel"),
        name="mixer_out_ln1",
    )(o, *tail, x, w, g, b)


def _gla_proj_kernel(x_ref, wq_ref, wk_ref, wv_ref, wg_ref, wa1_ref, wa2_ref, ba_ref,
                     q_ref, k_ref, v_ref, g_ref, la_ref):
    xb = x_ref[...].astype(BF16)
    q_ref[...] = (_dot(xb, wq_ref[...]) * (GLA_DK ** -0.5)).astype(BF16)
    k_ref[...] = _dot(xb, wk_ref[...]).astype(BF16)
    v_ref[...] = _dot(xb, wv_ref[...]).astype(BF16)
    gz = _dot(xb, wg_ref[...])
    g_ref[...] = (gz * _sigmoid(gz)).astype(BF16)
    z = _dot(_dot(xb, wa1_ref[...]).astype(BF16), wa2_ref[...]) + ba_ref[...]
    la_ref[...] = (jnp.minimum(z, 0.0) - jnp.log(1.0 + jnp.exp(-jnp.abs(z)))) * (1.0 / GLA_TAU)


def _gla_proj(x, w, tm):
    t = x.shape[0]
    row = lambda i: (i, 0)
    fix = lambda i: (0, 0)
    dk, dv = GLA_HEADS * GLA_DK, GLA_HEADS * GLA_DV
    return pl.pallas_call(
        _gla_proj_kernel,
        grid=(t // tm,),
        in_specs=[pl.BlockSpec((tm, D_MODEL), row)] + [pl.BlockSpec(a.shape, fix) for a in w],
        out_specs=[pl.BlockSpec((tm, dk), row), pl.BlockSpec((tm, dk), row), pl.BlockSpec((tm, dv), row),
                   pl.BlockSpec((tm, dv), row), pl.BlockSpec((tm, dk), row)],
        out_shape=[jax.ShapeDtypeStruct((t, dk), BF16), jax.ShapeDtypeStruct((t, dk), BF16),
                   jax.ShapeDtypeStruct((t, dv), BF16), jax.ShapeDtypeStruct((t, dv), BF16),
                   jax.ShapeDtypeStruct((t, dk), F32)],
        compiler_params=_params("parallel"),
        name="gla_proj",
    )(x, *w)


def _gla_chunk_kernel(q_ref, k_ref, v_ref, g_ref, la_ref, s0_ref, gn_ref, o_ref, st_ref, s_sc, *, nchunk):
    ti = pl.program_id(1)

    @pl.when(ti == 0)
    def _():
        s_sc[...] = s0_ref[0]

    c = CHUNK
    nsub = c // SUB
    r64 = lax.broadcasted_iota(I32, (c, c), 0)
    c64 = lax.broadcasted_iota(I32, (c, c), 1)
    tri = (r64 >= c64).astype(BF16)
    ones = jnp.ones((GLA_DK, LANES), BF16)
    lane16 = lax.broadcasted_iota(I32, (SUB, LANES), 1)
    row_l = lax.broadcasted_iota(I32, (c, LANES), 0)
    lane_l = lax.broadcasted_iota(I32, (c, LANES), 1)
    diag_vis = (lane_l // SUB == row_l // SUB) & (lane_l <= row_l)
    gn = gn_ref[...]

    def chunk(ci, carry):
        r0 = pl.multiple_of(ci * c, c)
        rows = pl.ds(r0, c)
        for h in range(GLA_HEADS):
            dk = slice(h * GLA_DK, (h + 1) * GLA_DK)
            dv = slice(h * GLA_DV, (h + 1) * GLA_DV)
            la = la_ref[rows, dk]
            la_hi = la.astype(BF16)
            la_lo = (la - la_hi.astype(F32)).astype(BF16)
            b = _dot(tri, la_hi) + _dot(tri, la_lo)
            q = q_ref[rows, dk].astype(F32)
            k = k_ref[rows, dk].astype(F32)
            v = v_ref[rows, dv]
            s_t = s_sc[h]
            o = _dot_nt((q * jnp.exp(b)).astype(BF16), s_t.astype(BF16))
            parts = []
            for s in range(c):
                i0 = (s // SUB) * SUB
                parts.append(q[i0:i0 + SUB] * (k[s:s + 1] * jnp.exp(b[i0:i0 + SUB] - b[s:s + 1])))
            rs = _dot(jnp.concatenate(parts, 0).astype(BF16), ones)
            blocks = []
            for i in range(nsub):
                dm = jnp.zeros((SUB, LANES), F32)
                for ss in range(SUB):
                    s = SUB * i + ss
                    dm = jnp.where(lane16 == s, rs[s * SUB:(s + 1) * SUB], dm)
                blocks.append(dm)
            att = jnp.where(diag_vis, jnp.concatenate(blocks, 0), 0.0)[:, :c]
            rk = jnp.concatenate([jnp.broadcast_to(b[SUB * j + SUB - 1:SUB * j + SUB], (SUB, GLA_DK))
                                  for j in range(nsub)], 0)
            kt = (k * jnp.exp(rk - b)).astype(BF16)
            for j in range(nsub - 1):
                rj = b[SUB * j + SUB - 1:SUB * j + SUB]
                qj = (q * jnp.exp(jnp.minimum(b - rj, 0.0))).astype(BF16)
                att = jnp.where((r64 >= SUB * (j + 1)) & (c64 // SUB == j), _dot_nt(qj, kt), att)
            o = o + _dot(att.astype(BF16), v)
            bend = b[c - 1:c]
            kh = (k * jnp.exp(bend - b)).astype(BF16)
            s_sc[h] = s_t * jnp.exp(bend) + _dot_tn(v, kh)
            on = _rms(o, gn)
            o_ref[rows, dv] = (on * g_ref[rows, dv].astype(F32)).astype(BF16)
        return carry

    lax.fori_loop(0, nchunk, chunk, 0)

    @pl.when(ti == pl.num_programs(1) - 1)
    def _():
        st_ref[0] = s_sc[...]


def _gla_chunks(q, k, v, g, la, s0t, gn, *, nb, nt, tb, blk0):
    dk, dv = GLA_HEADS * GLA_DK, GLA_HEADS * GLA_DV
    row = lambda b, t: (blk0 + b * nt + t, 0)
    st = lambda b, t: (b, 0, 0, 0)
    return pl.pallas_call(
        functools.partial(_gla_chunk_kernel, nchunk=tb // CHUNK),
        grid=(nb, nt),
        in_specs=[pl.BlockSpec((tb, dk), row), pl.BlockSpec((tb, dk), row), pl.BlockSpec((tb, dv), row),
                  pl.BlockSpec((tb, dv), row), pl.BlockSpec((tb, dk), row),
                  pl.BlockSpec((1, GLA_HEADS, GLA_DV, GLA_DK), st), pl.BlockSpec(gn.shape, lambda b, t: (0, 0))],
        out_specs=[pl.BlockSpec((tb, dv), lambda b, t: (b * nt + t, 0)),
                   pl.BlockSpec((1, GLA_HEADS, GLA_DV, GLA_DK), st)],
        out_shape=[jax.ShapeDtypeStruct((nb * nt * tb, dv), BF16),
                   jax.ShapeDtypeStruct((nb, GLA_HEADS, GLA_DV, GLA_DK), F32)],
        scratch_shapes=[pltpu.VMEM((GLA_HEADS, GLA_DV, GLA_DK), F32)],
        compiler_params=_params("parallel", "arbitrary"),
        name="gla_chunks",
    )(q, k, v, g, la, s0t, gn)


def _router_kernel(y_ref, wh_ref, wl_ref, bias_ref, u_ref, idx_ref, w_ref, rank_ref, cnt_ref, carry_sc):
    @pl.when(pl.program_id(0) == 0)
    def _():
        carry_sc[...] = jnp.zeros(carry_sc.shape, F32)

    y = y_ref[...]
    tm = y.shape[0]
    yh = y.astype(BF16)
    yl = (y - yh.astype(F32)).astype(BF16)
    wh, wl = wh_ref[...], wl_ref[...]
    logit = _dot_nt(wh, yh) + (_dot_nt(wh, yl) + _dot_nt(wl, yh))
    score = _sigmoid(logit)
    sel = score + bias_ref[...]
    shp = (N_GROUPS, GROUP_SIZE, tm)
    x3 = sel.reshape(shp)
    sc3 = score.reshape(shp)
    eidx = lax.broadcasted_iota(I32, shp, 1).astype(F32)
    gidx = lax.broadcasted_iota(I32, shp, 0).astype(F32)
    eflat = gidx * GROUP_SIZE + eidx
    neg = -jnp.inf
    m1 = jnp.max(x3, 1, keepdims=True)
    i1 = jnp.min(jnp.where(x3 == m1, eidx, float(GROUP_SIZE)), 1, keepdims=True)
    m2 = jnp.max(jnp.where(eidx == i1, neg, x3), 1, keepdims=True)
    gs = m1 + m2
    g1 = lax.broadcasted_iota(I32, (N_GROUPS, 1, tm), 0).astype(F32)
    gsel = jnp.zeros((N_GROUPS, 1, tm), F32)
    for _ in range(TOPK_GROUPS):
        m = jnp.max(gs, 0, keepdims=True)
        gi = jnp.min(jnp.where(gs == m, g1, float(N_GROUPS)), 0, keepdims=True)
        pick = g1 == gi
        gsel = jnp.where(pick, 1.0, gsel)
        gs = jnp.where(pick, neg, gs)
    masked = jnp.where(gsel > 0.0, x3, neg)
    chosen = jnp.zeros(shp, F32)
    idxs, ws = [], []
    for _ in range(TOP_K):
        m = jnp.max(jnp.max(masked, 0, keepdims=True), 1, keepdims=True)
        ei = jnp.min(jnp.min(jnp.where(masked == m, eflat, float(N_EXPERTS)), 0, keepdims=True), 1, keepdims=True)
        pick = eflat == ei
        ws.append(jnp.sum(jnp.sum(jnp.where(pick, sc3, 0.0), 0, keepdims=True), 1, keepdims=True))
        idxs.append(ei)
        masked = jnp.where(pick, neg, masked)
        chosen = jnp.where(pick, 1.0, chosen)
    wsum = ws[0]
    for wk in ws[1:]:
        wsum = wsum + wk
    m2d = chosen.reshape(N_EXPERTS, tm)
    before = _dot(m2d.astype(BF16), u_ref[...]) + carry_sc[:, :1]
    b3 = before.reshape(shp)
    for kk in range(TOP_K):
        rk = jnp.sum(jnp.sum(jnp.where(eflat == idxs[kk], b3, 0.0), 0, keepdims=True), 1, keepdims=True)
        idx_ref[kk:kk + 1, :] = idxs[kk].reshape(1, tm).astype(I32)
        rank_ref[kk:kk + 1, :] = rk.reshape(1, tm).astype(I32)
        w_ref[kk:kk + 1, :] = (ws[kk] / wsum * ROUTED_SCALE).reshape(1, tm)
    carry_sc[...] = carry_sc[...] + jnp.sum(m2d, -1, keepdims=True)
    cnt_ref[...] = carry_sc[...]


def _router(y, wh, wl, bias, upper, tm):
    t = y.shape[0]
    col = lambda i: (0, i)
    fix = lambda i: (0, 0)
    return pl.pallas_call(
        _router_kernel,
        grid=(t // tm,),
        in_specs=[pl.BlockSpec((tm, D_MODEL), lambda i: (i, 0)), pl.BlockSpec(wh.shape, fix),
                  pl.BlockSpec(wl.shape, fix), pl.BlockSpec(bias.shape, fix), pl.BlockSpec(upper.shape, fix)],
        out_specs=[pl.BlockSpec((TOP_K, tm), col), pl.BlockSpec((TOP_K, tm), col), pl.BlockSpec((TOP_K, tm), col),
                   pl.BlockSpec((N_EXPERTS, LANES), fix)],
        out_shape=[jax.ShapeDtypeStruct((TOP_K, t), I32), jax.ShapeDtypeStruct((TOP_K, t), F32),
                   jax.ShapeDtypeStruct((TOP_K, t), I32), jax.ShapeDtypeStruct((N_EXPERTS, LANES), F32)],
        scratch_shapes=[pltpu.VMEM((N_EXPERTS, LANES), F32)],
        compiler_params=_params("arbitrary"),
        name="moe_router",
    )(y, wh, wl, bias, upper)


def _sc_mesh():
    return plsc.VectorSubcoreMesh(core_axis_name="core", subcore_axis_name="subcore")


def _sc_scatter_rows(xa, xb, pos, n_out):
    t, c = xa.shape
    k = pos.shape[0]
    out = jax.ShapeDtypeStruct((n_out, c), xa.dtype)

    @functools.partial(pl.kernel, out_type=(out, out), mesh=_sc_mesh(), scratch_types=[])
    def scatter(xa_hbm, xb_hbm, i_hbm, oa_hbm, ob_hbm):
        for x_hbm, o_hbm in ((xa_hbm, oa_hbm), (xb_hbm, ob_hbm)):
            def body(x_vmem, i_vmem, o_hbm=o_hbm):
                for s in range(k):
                    pltpu.sync_copy(x_vmem, o_hbm.at[i_vmem.at[s]])

            pltpu.emit_pipeline(
                body,
                grid=(t // SC_WINDOW,),
                in_specs=[pl.BlockSpec((SC_WINDOW, c), lambda i: (i, 0)),
                          pl.BlockSpec((k, SC_WINDOW), lambda i: (0, i))],
                out_specs=[],
                core_axis_name=("core", "subcore"),
                dimension_semantics=(pltpu.PARALLEL,),
                trace_scopes=False,
            )(x_hbm, i_hbm)

    return scatter(xa, xb, pos)


def _sc_gather_rows(ya, yb, pos):
    k, t = pos.shape
    c = ya.shape[1]
    n = k * t
    idx = pos.reshape(1, n)
    out = jax.ShapeDtypeStruct((n, c), ya.dtype)

    @functools.partial(pl.kernel, out_type=(out, out), mesh=_sc_mesh(), scratch_types=[])
    def gather(ya_hbm, yb_hbm, i_hbm, oa_hbm, ob_hbm):
        for y_hbm, o_hbm in ((ya_hbm, oa_hbm), (yb_hbm, ob_hbm)):
            def body(i_vmem, o_vmem, y_hbm=y_hbm):
                pltpu.sync_copy(y_hbm.at[i_vmem.at[0]], o_vmem)

            pltpu.emit_pipeline(
                body,
                grid=(n // SC_WINDOW,),
                in_specs=[pl.BlockSpec((1, SC_WINDOW), lambda i: (0, i))],
                out_specs=[pl.BlockSpec((SC_WINDOW, c), lambda i: (i, 0))],
                core_axis_name=("core", "subcore"),
                dimension_semantics=(pltpu.PARALLEL,),
                trace_scopes=False,
            )(i_hbm, o_hbm)

    return gather(ya, yb, idx)


def _expert_kernel(te_ref, nu_ref, xa_ref, xb_ref, wg_ref, wu_ref, wd_ref, ya_ref, yb_ref, wgu_sc, wd_sc):
    j = pl.program_id(0)

    @pl.when(j < nu_ref[0])
    def _():
        @pl.when((j == 0) | (te_ref[j] != te_ref[jnp.maximum(j - 1, 0)]))
        def _():
            wgu_sc[:, :EXPERT_FF] = wg_ref[0, 0].astype(BF16)
            wgu_sc[:, EXPERT_FF:] = wu_ref[0, 0].astype(BF16)
            wd_sc[...] = wd_ref[0, 0].astype(BF16)

        x = _unpack_halves(xa_ref[...], xb_ref[...]).astype(BF16)
        gu = _dot(x, wgu_sc[...])
        g, u = gu[:, :EXPERT_FF], gu[:, EXPERT_FF:]
        h = (g * _sigmoid(g) * u).astype(BF16)
        a, b = _pack_halves(_dot(h, wd_sc[...]))
        ya_ref[...] = a
        yb_ref[...] = b


def _experts(tile_expert, n_used, xa, xb, wg, wu, wd, layer, tr):
    n_rows = xa.shape[0]
    row = lambda j, te, nu: (jnp.minimum(j, nu[0] - 1), 0)
    wmap = lambda j, te, nu: (layer, te[jnp.minimum(j, nu[0] - 1)], 0, 0)
    grid_spec = pltpu.PrefetchScalarGridSpec(
        num_scalar_prefetch=2,
        grid=(n_rows // tr,),
        in_specs=[pl.BlockSpec((tr, HALF), row), pl.BlockSpec((tr, HALF), row),
                  pl.BlockSpec((1, 1, D_MODEL, EXPERT_FF), wmap), pl.BlockSpec((1, 1, D_MODEL, EXPERT_FF), wmap),
                  pl.BlockSpec((1, 1, EXPERT_FF, D_MODEL), wmap)],
        out_specs=[pl.BlockSpec((tr, HALF), row), pl.BlockSpec((tr, HALF), row)],
        scratch_shapes=[pltpu.VMEM((D_MODEL, 2 * EXPERT_FF), BF16), pltpu.VMEM((EXPERT_FF, D_MODEL), BF16)],
    )
    return pl.pallas_call(
        _expert_kernel,
        grid_spec=grid_spec,
        out_shape=[jax.ShapeDtypeStruct((n_rows, HALF), I32), jax.ShapeDtypeStruct((n_rows, HALF), I32)],
        compiler_params=_params("arbitrary"),
        name="moe_experts",
    )(tile_expert, n_used, xa, xb, wg, wu, wd)


def _combine_ln_kernel(x_ref, wt_ref, ya_ref, yb_ref, wsgu_ref, wsd_ref, g_ref, b_ref, y_ref):
    x = x_ref[...]
    gu = _dot(x.astype(BF16), wsgu_ref[...])
    g, u = gu[:, :EXPERT_FF], gu[:, EXPERT_FF:]
    acc = _dot((g * _sigmoid(g) * u).astype(BF16), wsd_ref[...])
    for k in range(TOP_K):
        acc = acc + wt_ref[:, k:k + 1] * _unpack_halves(ya_ref[k], yb_ref[k])
    y_ref[...] = _layer_norm(DEEPNORM_ALPHA * x + acc, g_ref[...], b_ref[...])


def _combine_ln(x, wt, yga, ygb, wsgu, wsd, g, b, tm):
    t = x.shape[0]
    row = lambda i: (i, 0)
    fix = lambda i: (0, 0)
    slot = lambda i: (0, i, 0)
    return pl.pallas_call(
        _combine_ln_kernel,
        grid=(t // tm,),
        in_specs=[pl.BlockSpec((tm, D_MODEL), row), pl.BlockSpec((tm, TOP_K), row),
                  pl.BlockSpec((TOP_K, tm, HALF), slot), pl.BlockSpec((TOP_K, tm, HALF), slot),
                  pl.BlockSpec(wsgu.shape, fix), pl.BlockSpec(wsd.shape, fix),
                  pl.BlockSpec(g.shape, fix), pl.BlockSpec(b.shape, fix)],
        out_specs=pl.BlockSpec((tm, D_MODEL), row),
        out_shape=jax.ShapeDtypeStruct((t, D_MODEL), F32),
        compiler_params=_params("parallel"),
        name="moe_combine_ln2",
    )(x, wt, yga.reshape(TOP_K, t, HALF), ygb.reshape(TOP_K, t, HALF), wsgu, wsd, g, b)


def _moe_layer(y, ya, yb, w, layer, tm, tr, upper):
    wrh, wrl, rbias, wg, wu, wd, wsgu, wsd, g2, b2 = w
    t = y.shape[0]
    idx, wts, rank, cnt = _router(y, wrh, wrl, rbias, upper, tm)
    counts = cnt[:, 0].astype(I32)
    ntile = (counts + (tr - 1)) // tr
    tile_end = jnp.cumsum(ntile)
    offs = (tile_end - ntile) * tr
    pos = rank
    for e in range(1, N_EXPERTS):
        pos = pos + jnp.where(idx == e, offs[e], 0)
    n_tiles = (t * TOP_K) // tr + N_EXPERTS
    tile_expert = jnp.minimum(jnp.sum((tile_end[None, :] <= jnp.arange(n_tiles, dtype=I32)[:, None]).astype(I32), 1),
                              N_EXPERTS - 1)
    n_used = tile_end[-1:].astype(I32)
    xsa, xsb = _sc_scatter_rows(ya, yb, pos, n_tiles * tr)
    ysa, ysb = _experts(tile_expert, n_used, xsa, xsb, wg, wu, wd, layer, tr)
    yga, ygb = _sc_gather_rows(ysa, ysb, pos)
    return _combine_ln(y, wts.T, yga, ygb, wsgu, wsd, g2, b2, tm)


def _rope_tables(seq, past, dec_seq, n_dec_rows):
    half = MLA_ROPE // 2
    inv = ROPE_THETA ** (-jnp.arange(half, dtype=F32) / half)
    pos = jnp.concatenate([jnp.arange(seq), past + (jnp.arange(n_dec_rows) % dec_seq)]).astype(F32)
    ang = pos[:, None] * inv[None, :]
    cos, sin = jnp.cos(ang), jnp.sin(ang)
    n = pos.shape[0]
    z = lambda w: jnp.zeros((n, w), F32)
    cos_t = jnp.concatenate([cos, cos, jnp.ones((n, MLA_NOPE), F32), z(HEAD_PAD - MLA_ROPE - MLA_NOPE)], 1)
    msin_t = jnp.concatenate([-sin, z(LANES - half)], 1)
    sin_t = jnp.concatenate([z(half), sin, z(LANES - 2 * half)], 1)
    return cos_t, msin_t, sin_t


def _mla_weights(w_dq, q_norm, w_uq, w_dkv, kv_norm, w_uk, w_uv, w_o):
    wq = (w_uq * (MLA_SCALE * LOG2E)).reshape(MLA_Q_LORA, MLA_HEADS, MLA_NOPE + MLA_ROPE)
    wq = jnp.concatenate([wq[..., MLA_NOPE:], wq[..., :MLA_NOPE],
                          jnp.zeros((MLA_Q_LORA, MLA_HEADS, HEAD_PAD - MLA_NOPE - MLA_ROPE), F32)], -1)
    wq = wq.reshape(MLA_Q_LORA, MLA_HEADS * HEAD_PAD).astype(BF16)
    wkc_lat = w_dkv[:, :MLA_KV_LORA].astype(BF16)
    wkp = jnp.pad(w_dkv[:, MLA_KV_LORA:], ((0, 0), (0, LANES - MLA_ROPE))).astype(BF16)
    wk = w_uk.reshape(MLA_KV_LORA, MLA_HEADS, MLA_NOPE)
    wk = jnp.concatenate([jnp.zeros((MLA_KV_LORA, MLA_HEADS, MLA_ROPE), F32), wk,
                          jnp.zeros((MLA_KV_LORA, MLA_HEADS, HEAD_PAD - MLA_NOPE - MLA_ROPE), F32)], -1)
    wk = wk.reshape(MLA_KV_LORA, MLA_HEADS * HEAD_PAD).astype(BF16)
    eye = jnp.concatenate([jnp.eye(MLA_ROPE, dtype=F32), jnp.zeros((MLA_ROPE, HEAD_PAD - MLA_ROPE), F32)], 1)
    wke = jnp.tile(eye, (1, MLA_HEADS)).astype(BF16)
    proj = (w_dq.astype(BF16), q_norm.reshape(1, -1), wq, wkc_lat, kv_norm.reshape(1, -1), wkp)
    expand = (wk, wke, w_uv.astype(BF16), w_uv.T.astype(BF16))
    return proj, expand, w_o.astype(BF16)


def _gla_weights(w_q, w_k, w_v, w_a1, w_a2, b_a, w_g):
    wa1 = jnp.pad(w_a1, ((0, 0), (0, LANES - GLA_GATE_RANK))).astype(BF16)
    wa2 = jnp.pad(w_a2, ((0, LANES - GLA_GATE_RANK), (0, 0))).astype(BF16)
    return (w_q.astype(BF16), w_k.astype(BF16), w_v.astype(BF16), w_g.astype(BF16), wa1, wa2, b_a.reshape(1, -1))


def _moe_weights(w_router, router_bias, w_gate, w_up, w_down, ws_gate, ws_up, ws_down, g2, b2):
    wr_t = w_router.T
    wrh = wr_t.astype(BF16)
    wrl = (wr_t - wrh.astype(F32)).astype(BF16)
    wsgu = jnp.concatenate([ws_gate, ws_up], -1).astype(BF16)
    return (wrh, wrl, router_bias.reshape(-1, 1), w_gate, w_up, w_down, wsgu, ws_down.astype(BF16),
            g2.reshape(1, -1), b2.reshape(1, -1))


def kernel(x_prompt, x_sample, cache_ckv, cache_kpe, state_gla, mla_w_dq, mla_q_norm, mla_w_uq, mla_w_dkv, mla_kv_norm, mla_w_uk, mla_w_uv, mla_w_o, gla_w_q, gla_w_k, gla_w_v, gla_w_a1, gla_w_a2, gla_b_a, gla_w_g, gla_g_norm, gla_w_o, ln1_g, ln1_b, ln2_g, ln2_b, moe_w_router, moe_router_bias, moe_w_gate, moe_w_up, moe_w_down, moe_ws_gate, moe_ws_up, moe_ws_down):
    nb, seq, _ = x_prompt.shape
    ndb, dec_seq, _ = x_sample.shape
    past = cache_ckv.shape[2]
    assert dec_seq == CHUNK and (past + dec_seq) % 16 == 0

    mla_w = [_mla_weights(mla_w_dq[j], mla_q_norm[j], mla_w_uq[j], mla_w_dkv[j], mla_kv_norm[j],
                          mla_w_uk[j], mla_w_uv[j], mla_w_o[j]) for j in range(mla_w_dq.shape[0])]
    gla_w = [(_gla_weights(gla_w_q[j], gla_w_k[j], gla_w_v[j], gla_w_a1[j], gla_w_a2[j], gla_b_a[j], gla_w_g[j]),
              gla_w_o[j].astype(BF16), gla_g_norm[j].reshape(1, -1)) for j in range(gla_w_q.shape[0])]
    moe_w = [_moe_weights(moe_w_router[i], moe_router_bias[i], moe_w_gate, moe_w_up, moe_w_down,
                          moe_ws_gate[i], moe_ws_up[i], moe_ws_down[i], ln2_g[i], ln2_b[i]) for i in range(DEPTH)]
    ln1 = [(ln1_g[i].reshape(1, -1), ln1_b[i].reshape(1, -1)) for i in range(DEPTH)]

    def pipeline(xp, xs):
        nbp = xp.shape[0]
        tp = nbp * seq
        ts = 0 if xs is None else ndb * dec_seq
        t = tp + ts
        tm = 512 if (t % 512 == 0 and seq % 512 == 0 and ts <= 512) else 128
        tq = min(512, seq)
        tr = 1024 if t >= 8192 else 128
        assert t % tm == 0 and tp % tm == 0 and ts % tm == 0 and tm % dec_seq == 0 and seq % tq == 0
        x = xp.reshape(tp, D_MODEL)
        if ts:
            x = jnp.concatenate([x, xs.reshape(ts, D_MODEL)], 0)
        tabs = _rope_tables(seq, past, dec_seq, max(ts, tm))
        upper = (jnp.arange(tm)[:, None] < jnp.arange(tm)[None, :]).astype(BF16)
        out = {k: [] for k in ("ckv_p", "kpe_p", "gla_p", "ckv_s", "kpe_s", "gla_s")}
        for i in range(DEPTH):
            j = i // 2
            if i % 2 == 0:
                proj_w, exp_w, wo = mla_w[j]
                q, ckv, kpe = _mla_proj(x, tabs, proj_w, tm, tp // tm, seq // tm)
                k_p, vt_p = _kv_expand(ckv, kpe, exp_w, tm, tp, True)
                o = _attention_prompt(q, k_p, vt_p, nb=nbp, seq=seq, tq=min(ATTN_BLOCK, seq))
                o_s = None
                out["ckv_p"].append(ckv[:tp].reshape(nbp, seq, -1))
                out["kpe_p"].append(kpe[:tp].reshape(nbp, seq, -1))
                if ts:
                    ckv_s, kpe_s = ckv[tp:].reshape(ndb, dec_seq, -1), kpe[tp:].reshape(ndb, dec_seq, -1)
                    c_all = jnp.concatenate([cache_ckv[j], ckv_s], 1).reshape(-1, MLA_KV_LORA)
                    p_all = jnp.concatenate([cache_kpe[j], kpe_s], 1).reshape(-1, MLA_ROPE)
                    n_all = c_all.shape[0]
                    k_s, v_s = _kv_expand(c_all, p_all, exp_w, 512 if n_all % 512 == 0 else past + dec_seq, n_all,
                                          False)
                    o_s = _attention_sample(q, k_s, v_s, nb=ndb, tq=dec_seq, kv_len=past + dec_seq,
                                            q_blk0=tp // dec_seq)
                    out["ckv_s"].append(ckv_s)
                    out["kpe_s"].append(kpe_s)
            else:
                gw, wo, gn = gla_w[j]
                q, k, v, g, la = _gla_proj(x, gw, tm)
                zeros = jnp.zeros((nbp, GLA_HEADS, GLA_DV, GLA_DK), F32)
                o, st_p = _gla_chunks(q, k, v, g, la, zeros, gn, nb=nbp, nt=seq // tq, tb=tq, blk0=0)
                o_s = None
                out["gla_p"].append(jnp.swapaxes(st_p, -1, -2))
                if ts:
                    s0 = jnp.swapaxes(state_gla[j], -1, -2)
                    o_s, st_s = _gla_chunks(q, k, v, g, la, s0, gn, nb=ndb, nt=1, tb=dec_seq, blk0=tp // dec_seq)
                    out["gla_s"].append(jnp.swapaxes(st_s, -1, -2))
            y1, ya, yb = _proj_ln(o, o_s, x, wo, ln1[i][0], ln1[i][1], tm, transposed=(i % 2 == 0))
            x = _moe_layer(y1, ya, yb, moe_w[i], i, tm, tr, upper)
        out["y_p"] = x[:tp].reshape(nbp, seq, D_MODEL)
        if ts:
            out["y_s"] = x[tp:].reshape(ndb, dec_seq, D_MODEL)
        return out

    if nb % 2 == 0 and nb >= 2:
        h = nb // 2
        a, b = pipeline(x_prompt[:h], x_sample), pipeline(x_prompt[h:], None)
        cat = lambda key: jnp.concatenate([jnp.stack(a[key]), jnp.stack(b[key])], 1)
        y_p = jnp.concatenate([a["y_p"], b["y_p"]], 0)
        ckv_p, kpe_p, gla_p = cat("ckv_p"), cat("kpe_p"), cat("gla_p")
    else:
        a = pipeline(x_prompt, x_sample)
        y_p = a["y_p"]
        ckv_p, kpe_p, gla_p = jnp.stack(a["ckv_p"]), jnp.stack(a["kpe_p"]), jnp.stack(a["gla_p"])
    return (y_p, a["y_s"], ckv_p, kpe_p, gla_p,
            jnp.stack(a["ckv_s"]), jnp.stack(a["kpe_s"]), jnp.stack(a["gla_s"]))
```

```python
import functools

import jax
import jax.numpy as jnp
from jax import lax
from jax.experimental import pallas as pl
from jax.experimental.pallas import tpu as pltpu
from jax.experimental.pallas import tpu_sc as plsc

F32, BF16, I32 = jnp.float32, jnp.bfloat16, jnp.int32

D_MODEL = 1024
DEPTH = 4
CHUNK = 64
MLA_HEADS = 16
MLA_NOPE = 64
MLA_ROPE = 32
MLA_V = 64
MLA_Q_LORA = 384
MLA_KV_LORA = 256
MLA_SCALE = (MLA_NOPE + MLA_ROPE) ** -0.5
LOG2E = 1.4426950408889634
ROPE_THETA = 10000.0
GLA_HEADS = 4
GLA_DK = 128
GLA_DV = 256
GLA_GATE_RANK = 16
GLA_TAU = 16.0
N_EXPERTS = 64
TOP_K = 8
N_GROUPS = 8
GROUP_SIZE = N_EXPERTS // N_GROUPS
TOPK_GROUPS = 4
EXPERT_FF = 256
ROUTED_SCALE = 2.5
DEEPNORM_ALPHA = (2.0 * DEPTH) ** 0.25
NORM_EPS = 1e-5

LANES = 128
HEAD_PAD = 128
SUB = 16
SC_WINDOW = 128
HALF = D_MODEL // 4
VMEM_LIMIT = 48 * 1024 * 1024


def _dot(a, b):
    return jnp.dot(a, b, preferred_element_type=F32)


def _dot_nt(a, b):
    return lax.dot_general(a, b, (((1,), (1,)), ((), ())), preferred_element_type=F32)


def _dot_tn(a, b):
    return lax.dot_general(a, b, (((0,), (0,)), ((), ())), preferred_element_type=F32)


def _params(*sem):
    return pltpu.CompilerParams(dimension_semantics=sem, vmem_limit_bytes=VMEM_LIMIT)


def _sigmoid(x):
    return 1.0 / (1.0 + jnp.exp(-x))


def _rms(x, g):
    return x * lax.rsqrt(jnp.mean(x * x, -1, keepdims=True) + NORM_EPS) * g


def _layer_norm(z, g, b):
    mu = jnp.mean(z, -1, keepdims=True)
    zc = z - mu
    var = jnp.mean(zc * zc, -1, keepdims=True)
    return zc * lax.rsqrt(var + NORM_EPS) * g + b


def _rope128(x, cos, msin, sin):
    return x * cos + pltpu.roll(x, LANES - 16, 1) * msin + pltpu.roll(x, 16, 1) * sin


def _pack_halves(y):
    half = D_MODEL // 2
    p = lax.bitcast_convert_type(pltpu.pack_elementwise([y[:, half:], y[:, :half]], packed_dtype=BF16), I32)
    return p[:, :HALF], p[:, HALF:]


def _unpack_halves(a, b):
    p = jnp.concatenate([a, b], 1)
    hi = pltpu.unpack_elementwise(p, index=1, packed_dtype=BF16, unpacked_dtype=F32)
    lo = pltpu.unpack_elementwise(p, index=0, packed_dtype=BF16, unpacked_dtype=F32)
    return jnp.concatenate([hi, lo], 1)


def _mla_proj_kernel(x_ref, cos_ref, msin_ref, sin_ref, wdq_ref, qn_ref, wuq_ref, wkc_ref, kvn_ref,
                     wkp_ref, q_ref, ckv_ref, kpe_ref):
    xb = x_ref[...].astype(BF16)
    cos, msin, sin = cos_ref[...], msin_ref[...], sin_ref[...]
    cq = _rms(_dot(xb, wdq_ref[...]), qn_ref[...]).astype(BF16)
    for hp in range(MLA_HEADS // 2):
        q2 = _dot(cq, wuq_ref[:, hp * 2 * HEAD_PAD:(hp + 1) * 2 * HEAD_PAD])
        for s in range(2):
            c0 = (2 * hp + s) * HEAD_PAD
            q_ref[:, c0:c0 + HEAD_PAD] = _rope128(q2[:, s * HEAD_PAD:(s + 1) * HEAD_PAD], cos, msin, sin).astype(BF16)
    ckv_ref[...] = _rms(_dot(xb, wkc_ref[...]), kvn_ref[...])
    kp = _rope128(_dot(xb, wkp_ref[...]), cos, msin, sin)
    kpe_ref[...] = kp[:, :MLA_ROPE]


def _mla_proj(x, tabs, w, tm, n_prompt_tiles, tiles_per_seq):
    t = x.shape[0]
    cos, msin, sin = tabs
    wdq, qn, wuq, wkc, kvn, wkp = w
    row = lambda i: (i, 0)
    fix = lambda i: (0, 0)
    tab = lambda i: (jnp.where(i < n_prompt_tiles, i % tiles_per_seq, tiles_per_seq), 0)
    full = lambda a: pl.BlockSpec(a.shape, fix)
    return pl.pallas_call(
        _mla_proj_kernel,
        grid=(t // tm,),
        in_specs=[pl.BlockSpec((tm, D_MODEL), row),
                  pl.BlockSpec((tm, LANES), tab), pl.BlockSpec((tm, LANES), tab), pl.BlockSpec((tm, LANES), tab),
                  full(wdq), full(qn), full(wuq), full(wkc), full(kvn), full(wkp)],
        out_specs=[pl.BlockSpec((tm, MLA_HEADS * HEAD_PAD), row),
                   pl.BlockSpec((tm, MLA_KV_LORA), row),
                   pl.BlockSpec((tm, MLA_ROPE), row)],
        out_shape=[jax.ShapeDtypeStruct((t, MLA_HEADS * HEAD_PAD), BF16),
                   jax.ShapeDtypeStruct((t, MLA_KV_LORA), F32),
                   jax.ShapeDtypeStruct((t, MLA_ROPE), F32)],
        compiler_params=_params("parallel"),
        name="mla_proj",
    )(x, cos, msin, sin, wdq, qn, wuq, wkc, kvn, wkp)


def _kv_expand_kernel(ckv_ref, kpe_ref, wkc_ref, wke_ref, wuv_ref, k_ref, v_ref, *, v_transposed):
    c = ckv_ref[...].astype(BF16)
    p = kpe_ref[...].astype(BF16)
    k_ref[...] = (_dot(c, wkc_ref[...]) + _dot(p, wke_ref[...])).astype(BF16)
    if v_transposed:
        v_ref[...] = _dot_nt(wuv_ref[...], c).astype(BF16)
    else:
        v_ref[...] = _dot(c, wuv_ref[...]).astype(BF16)


def _kv_expand(ckv, kpe, w, tm, n_rows, v_transposed):
    wkc, wke, wuv, wuv_t = w
    row = lambda i: (i, 0)
    fix = lambda i: (0, 0)
    full = lambda a: pl.BlockSpec(a.shape, fix)
    hv = MLA_HEADS * MLA_V
    if v_transposed:
        v_spec, v_shape, wv = pl.BlockSpec((hv, tm), lambda i: (0, i)), (hv, n_rows), wuv_t
    else:
        v_spec, v_shape, wv = pl.BlockSpec((tm, hv), row), (n_rows, hv), wuv
    return pl.pallas_call(
        functools.partial(_kv_expand_kernel, v_transposed=v_transposed),
        grid=(n_rows // tm,),
        in_specs=[pl.BlockSpec((tm, MLA_KV_LORA), row), pl.BlockSpec((tm, MLA_ROPE), row),
                  full(wkc), full(wke), full(wv)],
        out_specs=[pl.BlockSpec((tm, MLA_HEADS * HEAD_PAD), row), v_spec],
        out_shape=[jax.ShapeDtypeStruct((n_rows, MLA_HEADS * HEAD_PAD), BF16), jax.ShapeDtypeStruct(v_shape, BF16)],
        compiler_params=_params("parallel"),
        name="mla_kv_expand",
    )(ckv, kpe, wkc, wke, wv)


ATTN_HEADS = 4
ATTN_BLOCK = 1024
ATTN_KEY_BLOCK = 1024


def _attn_prompt_kernel(q_ref, k_ref, vt_ref, o_ref, m_sc, l_sc, acc_sc, *, tk):
    qi = pl.program_id(2)
    tq = q_ref.shape[0]
    m_sc[...] = jnp.full(m_sc.shape, -jnp.inf, F32)
    l_sc[...] = jnp.zeros(l_sc.shape, F32)
    acc_sc[...] = jnp.zeros(acc_sc.shape, F32)

    nsub = tq // tk

    def block(ki, diag):
        keys = pl.ds(pl.multiple_of(ki * tk, tk), tk)
        masked = diag is not None
        if masked:
            vis = ((lax.broadcasted_iota(I32, (tk, tq), 0) + diag * tk) // CHUNK) <= (lax.broadcasted_iota(I32, (tk, tq), 1) // CHUNK)
        for h in range(ATTN_HEADS):
            cols = slice(h * HEAD_PAD, (h + 1) * HEAD_PAD)
            s = _dot_nt(k_ref[keys, cols], q_ref[:, cols])
            if masked:
                s = jnp.where(vis, s, -jnp.inf)
            m_prev = m_sc[h]
            m_new = jnp.maximum(m_prev, jnp.max(s, 0, keepdims=True))
            a = jnp.exp2(m_prev - m_new)
            p = jnp.exp2(s - m_new)
            l_sc[h] = a * l_sc[h] + jnp.sum(p, 0, keepdims=True)
            pv = _dot(vt_ref[h * MLA_V:(h + 1) * MLA_V, keys], p.astype(BF16))
            acc_sc[h] = a * acc_sc[h] + pv
            m_sc[h] = m_new

    def body(ki, carry):
        block(ki, None)
        return carry

    lax.fori_loop(0, qi * nsub, body, 0)
    for j in range(nsub):
        block(qi * nsub + j, j)
    for h in range(ATTN_HEADS):
        o_ref[h * MLA_V:(h + 1) * MLA_V, :] = (acc_sc[h] / l_sc[h]).astype(BF16)


def _attention_prompt(q, k, vt, *, nb, seq, tq):
    hq = MLA_HEADS // ATTN_HEADS
    nq = seq // tq
    return pl.pallas_call(
        functools.partial(_attn_prompt_kernel, tk=min(ATTN_KEY_BLOCK, tq)),
        grid=(nb, hq, nq),
        in_specs=[pl.BlockSpec((tq, ATTN_HEADS * HEAD_PAD), lambda b, h, qi: (b * nq + qi, h)),
                  pl.BlockSpec((seq, ATTN_HEADS * HEAD_PAD), lambda b, h, qi: (b, h)),
                  pl.BlockSpec((ATTN_HEADS * MLA_V, seq), lambda b, h, qi: (h, b))],
        out_specs=pl.BlockSpec((ATTN_HEADS * MLA_V, tq), lambda b, h, qi: (h, b * nq + qi)),
        out_shape=jax.ShapeDtypeStruct((MLA_HEADS * MLA_V, nb * seq), BF16),
        scratch_shapes=[pltpu.VMEM((ATTN_HEADS, 1, tq), F32), pltpu.VMEM((ATTN_HEADS, 1, tq), F32),
                        pltpu.VMEM((ATTN_HEADS, MLA_V, tq), F32)],
        compiler_params=_params("parallel", "parallel", "arbitrary"),
        name="mla_attention_causal",
    )(q, k, vt)


def _attn_sample_kernel(q_ref, k_ref, v_ref, o_ref):
    tq = q_ref.shape[0]
    v = v_ref[...]
    lane_head = lax.broadcasted_iota(I32, (tq, ATTN_HEADS * MLA_V), 1) // MLA_V
    o = jnp.zeros((tq, ATTN_HEADS * MLA_V), F32)
    for h in range(ATTN_HEADS):
        cols = slice(h * HEAD_PAD, (h + 1) * HEAD_PAD)
        s = _dot_nt(q_ref[:, cols], k_ref[:, cols])
        p = jnp.exp2(s - jnp.max(s, -1, keepdims=True))
        pv = _dot(p.astype(BF16), v) / jnp.sum(p, -1, keepdims=True)
        o = jnp.where(lane_head == h, pv, o)
    o_ref[...] = o.astype(BF16)


def _attention_sample(q, k, v, *, nb, tq, kv_len, q_blk0):
    hq = MLA_HEADS // ATTN_HEADS
    return pl.pallas_call(
        _attn_sample_kernel,
        grid=(nb, hq),
        in_specs=[pl.BlockSpec((tq, ATTN_HEADS * HEAD_PAD), lambda b, h: (q_blk0 + b, h)),
                  pl.BlockSpec((kv_len, ATTN_HEADS * HEAD_PAD), lambda b, h: (b, h)),
                  pl.BlockSpec((kv_len, ATTN_HEADS * MLA_V), lambda b, h: (b, h))],
        out_specs=pl.BlockSpec((tq, ATTN_HEADS * MLA_V), lambda b, h: (b, h)),
        out_shape=jax.ShapeDtypeStruct((nb * tq, MLA_HEADS * MLA_V), BF16),
        compiler_params=_params("parallel", "parallel"),
        name="mla_attention_full",
    )(q, k, v)


def _proj_ln_kernel(*refs, transposed, n_main, has_tail):
    if has_tail:
        o_ref, ot_ref, x_ref, w_ref, g_ref, b_ref, y_ref, ya_ref, yb_ref = refs
    else:
        o_ref, x_ref, w_ref, g_ref, b_ref, y_ref, ya_ref, yb_ref = refs

    def finish(h):
        y = _layer_norm(DEEPNORM_ALPHA * x_ref[...] + h, g_ref[...], b_ref[...])
        y_ref[...] = y
        a, b = _pack_halves(y)
        ya_ref[...] = a
        yb_ref[...] = b

    def main():
        finish(_dot_tn(o_ref[...], w_ref[...]) if transposed else _dot(o_ref[...], w_ref[...]))

    if has_tail:
        pl.when(pl.program_id(0) < n_main)(main)
        pl.when(pl.program_id(0) >= n_main)(lambda: finish(_dot(ot_ref[...], w_ref[...])))
    else:
        main()


def _proj_ln(o, o_tail, x, w, g, b, tm, transposed):
    t = x.shape[0]
    n_main = (o.shape[1] if transposed else o.shape[0]) // tm
    has_tail = o_tail is not None
    row = lambda i: (i, 0)
    fix = lambda i: (0, 0)
    if transposed:
        o_spec = pl.BlockSpec((D_MODEL, tm), lambda i: (0, jnp.minimum(i, n_main - 1)))
    else:
        o_spec = pl.BlockSpec((tm, D_MODEL), lambda i: (jnp.minimum(i, n_main - 1), 0))
    tail = [o_tail] if has_tail else []
    tail_spec = [pl.BlockSpec((tm, D_MODEL), lambda i: (jnp.maximum(i - n_main, 0), 0))] if has_tail else []
    return pl.pallas_call(
        functools.partial(_proj_ln_kernel, transposed=transposed, n_main=n_main, has_tail=has_tail),
        grid=(t // tm,),
        in_specs=[o_spec] + tail_spec + [pl.BlockSpec((tm, D_MODEL), row),
                  pl.BlockSpec(w.shape, fix), pl.BlockSpec(g.shape, fix), pl.BlockSpec(b.shape, fix)],
        out_specs=[pl.BlockSpec((tm, D_MODEL), row), pl.BlockSpec((tm, HALF), row), pl.BlockSpec((tm, HALF), row)],
        out_shape=[jax.ShapeDtypeStruct((t, D_MODEL), F32), jax.ShapeDtypeStruct((t, HALF), I32),
                   jax.ShapeDtypeStruct((t, HALF), I32)],
        compiler_params=_params("parallel"),
        name="mixer_out_ln1",
    )(o, *tail, x, w, g, b)


def _gla_proj_kernel(x_ref, wq_ref, wk_ref, wv_ref, wg_ref, wa1_ref, wa2_ref, ba_ref,
                     q_ref, k_ref, v_ref, g_ref, la_ref):
    xb = x_ref[...].astype(BF16)
    q_ref[...] = (_dot(xb, wq_ref[...]) * (GLA_DK ** -0.5)).astype(BF16)
    k_ref[...] = _dot(xb, wk_ref[...]).astype(BF16)
    v_ref[...] = _dot(xb, wv_ref[...]).astype(BF16)
    gz = _dot(xb, wg_ref[...])
    g_ref[...] = (gz * _sigmoid(gz)).astype(BF16)
    z = _dot(_dot(xb, wa1_ref[...]).astype(BF16), wa2_ref[...]) + ba_ref[...]
    la_ref[...] = (jnp.minimum(z, 0.0) - jnp.log(1.0 + jnp.exp(-jnp.abs(z)))) * (1.0 / GLA_TAU)


def _gla_proj(x, w, tm):
    t = x.shape[0]
    row = lambda i: (i, 0)
    fix = lambda i: (0, 0)
    dk, dv = GLA_HEADS * GLA_DK, GLA_HEADS * GLA_DV
    return pl.pallas_call(
        _gla_proj_kernel,
        grid=(t // tm,),
        in_specs=[pl.BlockSpec((tm, D_MODEL), row)] + [pl.BlockSpec(a.shape, fix) for a in w],
        out_specs=[pl.BlockSpec((tm, dk), row), pl.BlockSpec((tm, dk), row), pl.BlockSpec((tm, dv), row),
                   pl.BlockSpec((tm, dv), row), pl.BlockSpec((tm, dk), row)],
        out_shape=[jax.ShapeDtypeStruct((t, dk), BF16), jax.ShapeDtypeStruct((t, dk), BF16),
                   jax.ShapeDtypeStruct((t, dv), BF16), jax.ShapeDtypeStruct((t, dv), BF16),
                   jax.ShapeDtypeStruct((t, dk), F32)],
        compiler_params=_params("parallel"),
        name="gla_proj",
    )(x, *w)


def _gla_chunk_kernel(q_ref, k_ref, v_ref, g_ref, la_ref, s0_ref, gn_ref, o_ref, st_ref, s_sc, *, nchunk):
    ti = pl.program_id(1)

    @pl.when(ti == 0)
    def _():
        s_sc[...] = s0_ref[0]

    c = CHUNK
    nsub = c // SUB
    r64 = lax.broadcasted_iota(I32, (c, c), 0)
    c64 = lax.broadcasted_iota(I32, (c, c), 1)
    tri = (r64 >= c64).astype(BF16)
    ones = jnp.ones((GLA_DK, LANES), BF16)
    lane16 = lax.broadcasted_iota(I32, (SUB, LANES), 1)
    row_l = lax.broadcasted_iota(I32, (c, LANES), 0)
    lane_l = lax.broadcasted_iota(I32, (c, LANES), 1)
    diag_vis = (lane_l // SUB == row_l // SUB) & (lane_l <= row_l)
    gn = gn_ref[...]

    def chunk(ci, carry):
        r0 = pl.multiple_of(ci * c, c)
        rows = pl.ds(r0, c)
        for h in range(GLA_HEADS):
            dk = slice(h * GLA_DK, (h + 1) * GLA_DK)
            dv = slice(h * GLA_DV, (h + 1) * GLA_DV)
            la = la_ref[rows, dk]
            la_hi = la.astype(BF16)
            la_lo = (la - la_hi.astype(F32)).astype(BF16)
            b = _dot(tri, la_hi) + _dot(tri, la_lo)
            q = q_ref[rows, dk].astype(F32)
            k = k_ref[rows, dk].astype(F32)
            v = v_ref[rows, dv]
            s_t = s_sc[h]
            o = _dot_nt((q * jnp.exp(b)).astype(BF16), s_t.astype(BF16))
            parts = []
            for s in range(c):
                i0 = (s // SUB) * SUB
                parts.append(q[i0:i0 + SUB] * (k[s:s + 1] * jnp.exp(b[i0:i0 + SUB] - b[s:s + 1])))
            rs = _dot(jnp.concatenate(parts, 0).astype(BF16), ones)
            blocks = []
            for i in range(nsub):
                dm = jnp.zeros((SUB, LANES), F32)
                for ss in range(SUB):
                    s = SUB * i + ss
                    dm = jnp.where(lane16 == s, rs[s * SUB:(s + 1) * SUB], dm)
                blocks.append(dm)
            att = jnp.where(diag_vis, jnp.concatenate(blocks, 0), 0.0)[:, :c]
            rk = jnp.concatenate([jnp.broadcast_to(b[SUB * j + SUB - 1:SUB * j + SUB], (SUB, GLA_DK))
                                  for j in range(nsub)], 0)
            kt = (k * jnp.exp(rk - b)).astype(BF16)
            for j in range(nsub - 1):
                rj = b[SUB * j + SUB - 1:SUB * j + SUB]
                qj = (q * jnp.exp(jnp.minimum(b - rj, 0.0))).astype(BF16)
                att = jnp.where((r64 >= SUB * (j + 1)) & (c64 // SUB == j), _dot_nt(qj, kt), att)
            o = o + _dot(att.astype(BF16), v)
            bend = b[c - 1:c]
            kh = (k * jnp.exp(bend - b)).astype(BF16)
            s_sc[h] = s_t * jnp.exp(bend) + _dot_tn(v, kh)
            on = _rms(o, gn)
            o_ref[rows, dv] = (on * g_ref[rows, dv].astype(F32)).astype(BF16)
        return carry

    lax.fori_loop(0, nchunk, chunk, 0)

    @pl.when(ti == pl.num_programs(1) - 1)
    def _():
        st_ref[0] = s_sc[...]


def _gla_chunks(q, k, v, g, la, s0t, gn, *, nb, nt, tb, blk0):
    dk, dv = GLA_HEADS * GLA_DK, GLA_HEADS * GLA_DV
    row = lambda b, t: (blk0 + b * nt + t, 0)
    st = lambda b, t: (b, 0, 0, 0)
    return pl.pallas_call(
        functools.partial(_gla_chunk_kernel, nchunk=tb // CHUNK),
        grid=(nb, nt),
        in_specs=[pl.BlockSpec((tb, dk), row), pl.BlockSpec((tb, dk), row), pl.BlockSpec((tb, dv), row),
                  pl.BlockSpec((tb, dv), row), pl.BlockSpec((tb, dk), row),
                  pl.BlockSpec((1, GLA_HEADS, GLA_DV, GLA_DK), st), pl.BlockSpec(gn.shape, lambda b, t: (0, 0))],
        out_specs=[pl.BlockSpec((tb, dv), lambda b, t: (b * nt + t, 0)),
                   pl.BlockSpec((1, GLA_HEADS, GLA_DV, GLA_DK), st)],
        out_shape=[jax.ShapeDtypeStruct((nb * nt * tb, dv), BF16),
                   jax.ShapeDtypeStruct((nb, GLA_HEADS, GLA_DV, GLA_DK), F32)],
        scratch_shapes=[pltpu.VMEM((GLA_HEADS, GLA_DV, GLA_DK), F32)],
        compiler_params=_params("parallel", "arbitrary"),
        name="gla_chunks",
    )(q, k, v, g, la, s0t, gn)


def _router_kernel(y_ref, wh_ref, wl_ref, bias_ref, u_ref, idx_ref, w_ref, rank_ref, cnt_ref, carry_sc):
    @pl.when(pl.program_id(0) == 0)
    def _():
        carry_sc[...] = jnp.zeros(carry_sc.shape, F32)

    y = y_ref[...]
    tm = y.shape[0]
    yh = y.astype(BF16)
    yl = (y - yh.astype(F32)).astype(BF16)
    wh, wl = wh_ref[...], wl_ref[...]
    logit = _dot_nt(wh, yh) + (_dot_nt(wh, yl) + _dot_nt(wl, yh))
    score = _sigmoid(logit)
    sel = score + bias_ref[...]
    shp = (N_GROUPS, GROUP_SIZE, tm)
    x3 = sel.reshape(shp)
    sc3 = score.reshape(shp)
    eidx = lax.broadcasted_iota(I32, shp, 1).astype(F32)
    gidx = lax.broadcasted_iota(I32, shp, 0).astype(F32)
    eflat = gidx * GROUP_SIZE + eidx
    neg = -jnp.inf
    m1 = jnp.max(x3, 1, keepdims=True)
    i1 = jnp.min(jnp.where(x3 == m1, eidx, float(GROUP_SIZE)), 1, keepdims=True)
    m2 = jnp.max(jnp.where(eidx == i1, neg, x3), 1, keepdims=True)
    gs = m1 + m2
    g1 = lax.broadcasted_iota(I32, (N_GROUPS, 1, tm), 0).astype(F32)
    gsel = jnp.zeros((N_GROUPS, 1, tm), F32)
    for _ in range(TOPK_GROUPS):
        m = jnp.max(gs, 0, keepdims=True)
        gi = jnp.min(jnp.where(gs == m, g1, float(N_GROUPS)), 0, keepdims=True)
        pick = g1 == gi
        gsel = jnp.where(pick, 1.0, gsel)
        gs = jnp.where(pick, neg, gs)
    masked = jnp.where(gsel > 0.0, x3, neg)
    chosen = jnp.zeros(shp, F32)
    idxs, ws = [], []
    for _ in range(TOP_K):
        m = jnp.max(jnp.max(masked, 0, keepdims=True), 1, keepdims=True)
        ei = jnp.min(jnp.min(jnp.where(masked == m, eflat, float(N_EXPERTS)), 0, keepdims=True), 1, keepdims=True)
        pick = eflat == ei
        ws.append(jnp.sum(jnp.sum(jnp.where(pick, sc3, 0.0), 0, keepdims=True), 1, keepdims=True))
        idxs.append(ei)
        masked = jnp.where(pick, neg, masked)
        chosen = jnp.where(pick, 1.0, chosen)
    wsum = ws[0]
    for wk in ws[1:]:
        wsum = wsum + wk
    m2d = chosen.reshape(N_EXPERTS, tm)
    before = _dot(m2d.astype(BF16), u_ref[...]) + carry_sc[:, :1]
    b3 = before.reshape(shp)
    for kk in range(TOP_K):
        rk = jnp.sum(jnp.sum(jnp.where(eflat == idxs[kk], b3, 0.0), 0, keepdims=True), 1, keepdims=True)
        idx_ref[kk:kk + 1, :] = idxs[kk].reshape(1, tm).astype(I32)
        rank_ref[kk:kk + 1, :] = rk.reshape(1, tm).astype(I32)
        w_ref[kk:kk + 1, :] = (ws[kk] / wsum * ROUTED_SCALE).reshape(1, tm)
    carry_sc[...] = carry_sc[...] + jnp.sum(m2d, -1, keepdims=True)
    cnt_ref[...] = carry_sc[...]


def _router(y, wh, wl, bias, upper, tm):
    t = y.shape[0]
    col = lambda i: (0, i)
    fix = lambda i: (0, 0)
    return pl.pallas_call(
        _router_kernel,
        grid=(t // tm,),
        in_specs=[pl.BlockSpec((tm, D_MODEL), lambda i: (i, 0)), pl.BlockSpec(wh.shape, fix),
                  pl.BlockSpec(wl.shape, fix), pl.BlockSpec(bias.shape, fix), pl.BlockSpec(upper.shape, fix)],
        out_specs=[pl.BlockSpec((TOP_K, tm), col), pl.BlockSpec((TOP_K, tm), col), pl.BlockSpec((TOP_K, tm), col),
                   pl.BlockSpec((N_EXPERTS, LANES), fix)],
        out_shape=[jax.ShapeDtypeStruct((TOP_K, t), I32), jax.ShapeDtypeStruct((TOP_K, t), F32),
                   jax.ShapeDtypeStruct((TOP_K, t), I32), jax.ShapeDtypeStruct((N_EXPERTS, LANES), F32)],
        scratch_shapes=[pltpu.VMEM((N_EXPERTS, LANES), F32)],
        compiler_params=_params("arbitrary"),
        name="moe_router",
    )(y, wh, wl, bias, upper)


def _sc_mesh():
    return plsc.VectorSubcoreMesh(core_axis_name="core", subcore_axis_name="subcore")


def _sc_scatter_rows(xa, xb, pos, n_out):
    t, c = xa.shape
    k = pos.shape[0]
    out = jax.ShapeDtypeStruct((n_out, c), xa.dtype)

    @functools.partial(pl.kernel, out_type=(out, out), mesh=_sc_mesh(), scratch_types=[])
    def scatter(xa_hbm, xb_hbm, i_hbm, oa_hbm, ob_hbm):
        for x_hbm, o_hbm in ((xa_hbm, oa_hbm), (xb_hbm, ob_hbm)):
            def body(x_vmem, i_vmem, o_hbm=o_hbm):
                for s in range(k):
                    pltpu.sync_copy(x_vmem, o_hbm.at[i_vmem.at[s]])

            pltpu.emit_pipeline(
                body,
                grid=(t // SC_WINDOW,),
                in_specs=[pl.BlockSpec((SC_WINDOW, c), lambda i: (i, 0)),
                          pl.BlockSpec((k, SC_WINDOW), lambda i: (0, i))],
                out_specs=[],
                core_axis_name=("core", "subcore"),
                dimension_semantics=(pltpu.PARALLEL,),
                trace_scopes=False,
            )(x_hbm, i_hbm)

    return scatter(xa, xb, pos)


def _sc_gather_rows(ya, yb, pos):
    k, t = pos.shape
    c = ya.shape[1]
    n = k * t
    idx = pos.reshape(1, n)
    out = jax.ShapeDtypeStruct((n, c), ya.dtype)

    @functools.partial(pl.kernel, out_type=(out, out), mesh=_sc_mesh(), scratch_types=[])
    def gather(ya_hbm, yb_hbm, i_hbm, oa_hbm, ob_hbm):
        for y_hbm, o_hbm in ((ya_hbm, oa_hbm), (yb_hbm, ob_hbm)):
            def body(i_vmem, o_vmem, y_hbm=y_hbm):
                pltpu.sync_copy(y_hbm.at[i_vmem.at[0]], o_vmem)

            pltpu.emit_pipeline(
                body,
                grid=(n // SC_WINDOW,),
                in_specs=[pl.BlockSpec((1, SC_WINDOW), lambda i: (0, i))],
                out_specs=[pl.BlockSpec((SC_WINDOW, c), lambda i: (i, 0))],
                core_axis_name=("core", "subcore"),
                dimension_semantics=(pltpu.PARALLEL,),
                trace_scopes=False,
            )(i_hbm, o_hbm)

    return gather(ya, yb, idx)


def _expert_kernel(te_ref, nu_ref, xa_ref, xb_ref, wg_ref, wu_ref, wd_ref, ya_ref, yb_ref, wgu_sc, wd_sc):
    j = pl.program_id(0)

    @pl.when(j < nu_ref[0])
    def _():
        @pl.when((j == 0) | (te_ref[j] != te_ref[jnp.maximum(j - 1, 0)]))
        def _():
            wgu_sc[:, :EXPERT_FF] = wg_ref[0, 0].astype(BF16)
            wgu_sc[:, EXPERT_FF:] = wu_ref[0, 0].astype(BF16)
            wd_sc[...] = wd_ref[0, 0].astype(BF16)

        x = _unpack_halves(xa_ref[...], xb_ref[...]).astype(BF16)
        gu = _dot(x, wgu_sc[...])
        g, u = gu[:, :EXPERT_FF], gu[:, EXPERT_FF:]
        h = (g * _sigmoid(g) * u).astype(BF16)
        a, b = _pack_halves(_dot(h, wd_sc[...]))
        ya_ref[...] = a
        yb_ref[...] = b


def _experts(tile_expert, n_used, xa, xb, wg, wu, wd, layer, tr):
    n_rows = xa.shape[0]
    row = lambda j, te, nu: (jnp.minimum(j, nu[0] - 1), 0)
    wmap = lambda j, te, nu: (layer, te[jnp.minimum(j, nu[0] - 1)], 0, 0)
    grid_spec = pltpu.PrefetchScalarGridSpec(
        num_scalar_prefetch=2,
        grid=(n_rows // tr,),
        in_specs=[pl.BlockSpec((tr, HALF), row), pl.BlockSpec((tr, HALF), row),
                  pl.BlockSpec((1, 1, D_MODEL, EXPERT_FF), wmap), pl.BlockSpec((1, 1, D_MODEL, EXPERT_FF), wmap),
                  pl.BlockSpec((1, 1, EXPERT_FF, D_MODEL), wmap)],
        out_specs=[pl.BlockSpec((tr, HALF), row), pl.BlockSpec((tr, HALF), row)],
        scratch_shapes=[pltpu.VMEM((D_MODEL, 2 * EXPERT_FF), BF16), pltpu.VMEM((EXPERT_FF, D_MODEL), BF16)],
    )
    return pl.pallas_call(
        _expert_kernel,
        grid_spec=grid_spec,
        out_shape=[jax.ShapeDtypeStruct((n_rows, HALF), I32), jax.ShapeDtypeStruct((n_rows, HALF), I32)],
        compiler_params=_params("arbitrary"),
        name="moe_experts",
    )(tile_expert, n_used, xa, xb, wg, wu, wd)


def _combine_ln_kernel(x_ref, wt_ref, ya_ref, yb_ref, wsgu_ref, wsd_ref, g_ref, b_ref, y_ref):
    x = x_ref[...]
    gu = _dot(x.astype(BF16), wsgu_ref[...])
    g, u = gu[:, :EXPERT_FF], gu[:, EXPERT_FF:]
    acc = _dot((g * _sigmoid(g) * u).astype(BF16), wsd_ref[...])
    for k in range(TOP_K):
        acc = acc + wt_ref[:, k:k + 1] * _unpack_halves(ya_ref[k], yb_ref[k])
    y_ref[...] = _layer_norm(DEEPNORM_ALPHA * x + acc, g_ref[...], b_ref[...])


def _combine_ln(x, wt, yga, ygb, wsgu, wsd, g, b, tm):
    t = x.shape[0]
    row = lambda i: (i, 0)
    fix = lambda i: (0, 0)
    slot = lambda i: (0, i, 0)
    return pl.pallas_call(
        _combine_ln_kernel,
        grid=(t // tm,),
        in_specs=[pl.BlockSpec((tm, D_MODEL), row), pl.BlockSpec((tm, TOP_K), row),
                  pl.BlockSpec((TOP_K, tm, HALF), slot), pl.BlockSpec((TOP_K, tm, HALF), slot),
                  pl.BlockSpec(wsgu.shape, fix), pl.BlockSpec(wsd.shape, fix),
                  pl.BlockSpec(g.shape, fix), pl.BlockSpec(b.shape, fix)],
        out_specs=pl.BlockSpec((tm, D_MODEL), row),
        out_shape=jax.ShapeDtypeStruct((t, D_MODEL), F32),
        compiler_params=_params("parallel"),
        name="moe_combine_ln2",
    )(x, wt, yga.reshape(TOP_K, t, HALF), ygb.reshape(TOP_K, t, HALF), wsgu, wsd, g, b)


def _moe_layer(y, ya, yb, w, layer, tm, tr, upper):
    wrh, wrl, rbias, wg, wu, wd, wsgu, wsd, g2, b2 = w
    t = y.shape[0]
    idx, wts, rank, cnt = _router(y, wrh, wrl, rbias, upper, tm)
    counts = cnt[:, 0].astype(I32)
    ntile = (counts + (tr - 1)) // tr
    tile_end = jnp.cumsum(ntile)
    offs = (tile_end - ntile) * tr
    pos = rank
    for e in range(1, N_EXPERTS):
        pos = pos + jnp.where(idx == e, offs[e], 0)
    n_tiles = (t * TOP_K) // tr + N_EXPERTS
    tile_expert = jnp.minimum(jnp.sum((tile_end[None, :] <= jnp.arange(n_tiles, dtype=I32)[:, None]).astype(I32), 1),
                              N_EXPERTS - 1)
    n_used = tile_end[-1:].astype(I32)
    xsa, xsb = _sc_scatter_rows(ya, yb, pos, n_tiles * tr)
    ysa, ysb = _experts(tile_expert, n_used, xsa, xsb, wg, wu, wd, layer, tr)
    yga, ygb = _sc_gather_rows(ysa, ysb, pos)
    return _combine_ln(y, wts.T, yga, ygb, wsgu, wsd, g2, b2, tm)


def _rope_tables(seq, past, dec_seq, n_dec_rows):
    half = MLA_ROPE // 2
    inv = ROPE_THETA ** (-jnp.arange(half, dtype=F32) / half)
    pos = jnp.concatenate([jnp.arange(seq), past + (jnp.arange(n_dec_rows) % dec_seq)]).astype(F32)
    ang = pos[:, None] * inv[None, :]
    cos, sin = jnp.cos(ang), jnp.sin(ang)
    n = pos.shape[0]
    z = lambda w: jnp.zeros((n, w), F32)
    cos_t = jnp.concatenate([cos, cos, jnp.ones((n, MLA_NOPE), F32), z(HEAD_PAD - MLA_ROPE - MLA_NOPE)], 1)
    msin_t = jnp.concatenate([-sin, z(LANES - half)], 1)
    sin_t = jnp.concatenate([z(half), sin, z(LANES - 2 * half)], 1)
    return cos_t, msin_t, sin_t


def _mla_weights(w_dq, q_norm, w_uq, w_dkv, kv_norm, w_uk, w_uv, w_o):
    wq = (w_uq * (MLA_SCALE * LOG2E)).reshape(MLA_Q_LORA, MLA_HEADS, MLA_NOPE + MLA_ROPE)
    wq = jnp.concatenate([wq[..., MLA_NOPE:], wq[..., :MLA_NOPE],
                          jnp.zeros((MLA_Q_LORA, MLA_HEADS, HEAD_PAD - MLA_NOPE - MLA_ROPE), F32)], -1)
    wq = wq.reshape(MLA_Q_LORA, MLA_HEADS * HEAD_PAD).astype(BF16)
    wkc_lat = w_dkv[:, :MLA_KV_LORA].astype(BF16)
    wkp = jnp.pad(w_dkv[:, MLA_KV_LORA:], ((0, 0), (0, LANES - MLA_ROPE))).astype(BF16)
    wk = w_uk.reshape(MLA_KV_LORA, MLA_HEADS, MLA_NOPE)
    wk = jnp.concatenate([jnp.zeros((MLA_KV_LORA, MLA_HEADS, MLA_ROPE), F32), wk,
                          jnp.zeros((MLA_KV_LORA, MLA_HEADS, HEAD_PAD - MLA_NOPE - MLA_ROPE), F32)], -1)
    wk = wk.reshape(MLA_KV_LORA, MLA_HEADS * HEAD_PAD).astype(BF16)
    eye = jnp.concatenate([jnp.eye(MLA_ROPE, dtype=F32), jnp.zeros((MLA_ROPE, HEAD_PAD - MLA_ROPE), F32)], 1)
    wke = jnp.tile(eye, (1, MLA_HEADS)).astype(BF16)
    proj = (w_dq.astype(BF16), q_norm.reshape(1, -1), wq, wkc_lat, kv_norm.reshape(1, -1), wkp)
    expand = (wk, wke, w_uv.astype(BF16), w_uv.T.astype(BF16))
    return proj, expand, w_o.astype(BF16)


def _gla_weights(w_q, w_k, w_v, w_a1, w_a2, b_a, w_g):
    wa1 = jnp.pad(w_a1, ((0, 0), (0, LANES - GLA_GATE_RANK))).astype(BF16)
    wa2 = jnp.pad(w_a2, ((0, LANES - GLA_GATE_RANK), (0, 0))).astype(BF16)
    return (w_q.astype(BF16), w_k.astype(BF16), w_v.astype(BF16), w_g.astype(BF16), wa1, wa2, b_a.reshape(1, -1))


def _moe_weights(w_router, router_bias, w_gate, w_up, w_down, ws_gate, ws_up, ws_down, g2, b2):
    wr_t = w_router.T
    wrh = wr_t.astype(BF16)
    wrl = (wr_t - wrh.astype(F32)).astype(BF16)
    wsgu = jnp.concatenate([ws_gate, ws_up], -1).astype(BF16)
    return (wrh, wrl, router_bias.reshape(-1, 1), w_gate, w_up, w_down, wsgu, ws_down.astype(BF16),
            g2.reshape(1, -1), b2.reshape(1, -1))


def kernel(x_prompt, x_sample, cache_ckv, cache_kpe, state_gla, mla_w_dq, mla_q_norm, mla_w_uq, mla_w_dkv, mla_kv_norm, mla_w_uk, mla_w_uv, mla_w_o, gla_w_q, gla_w_k, gla_w_v, gla_w_a1, gla_w_a2, gla_b_a, gla_w_g, gla_g_norm, gla_w_o, ln1_g, ln1_b, ln2_g, ln2_b, moe_w_router, moe_router_bias, moe_w_gate, moe_w_up, moe_w_down, moe_ws_gate, moe_ws_up, moe_ws_down):
    nb, seq, _ = x_prompt.shape
    ndb, dec_seq, _ = x_sample.shape
    past = cache_ckv.shape[2]
    assert dec_seq == CHUNK and (past + dec_seq) % 16 == 0

    mla_w = [_mla_weights(mla_w_dq[j], mla_q_norm[j], mla_w_uq[j], mla_w_dkv[j], mla_kv_norm[j],
                          mla_w_uk[j], mla_w_uv[j], mla_w_o[j]) for j in range(mla_w_dq.shape[0])]
    gla_w = [(_gla_weights(gla_w_q[j], gla_w_k[j], gla_w_v[j], gla_w_a1[j], gla_w_a2[j], gla_b_a[j], gla_w_g[j]),
              gla_w_o[j].astype(BF16), gla_g_norm[j].reshape(1, -1)) for j in range(gla_w_q.shape[0])]
    moe_w = [_moe_weights(moe_w_router[i], moe_router_bias[i], moe_w_gate, moe_w_up, moe_w_down,
                          moe_ws_gate[i], moe_ws_up[i], moe_ws_down[i], ln2_g[i], ln2_b[i]) for i in range(DEPTH)]
    ln1 = [(ln1_g[i].reshape(1, -1), ln1_b[i].reshape(1, -1)) for i in range(DEPTH)]

    def pipeline(xp, xs):
        nbp = xp.shape[0]
        tp = nbp * seq
        ts = 0 if xs is None else ndb * dec_seq
        t = tp + ts
        tm = 512 if (t % 512 == 0 and seq % 512 == 0 and ts <= 512) else 128
        tq = min(512, seq)
        tr = 1024 if t >= 8192 else 128
        assert t % tm == 0 and tp % tm == 0 and ts % tm == 0 and tm % dec_seq == 0 and seq % tq == 0
        x = xp.reshape(tp, D_MODEL)
        if ts:
            x = jnp.concatenate([x, xs.reshape(ts, D_MODEL)], 0)
        tabs = _rope_tables(seq, past, dec_seq, max(ts, tm))
        upper = (jnp.arange(tm)[:, None] < jnp.arange(tm)[None, :]).astype(BF16)
        out = {k: [] for k in ("ckv_p", "kpe_p", "gla_p", "ckv_s", "kpe_s", "gla_s")}
        for i in range(DEPTH):
            j = i // 2
            if i % 2 == 0:
                proj_w, exp_w, wo = mla_w[j]
                q, ckv, kpe = _mla_proj(x, tabs, proj_w, tm, tp // tm, seq // tm)
                k_p, vt_p = _kv_expand(ckv, kpe, exp_w, tm, tp, True)
                o = _attention_prompt(q, k_p, vt_p, nb=nbp, seq=seq, tq=min(ATTN_BLOCK, seq))
                o_s = None
                out["ckv_p"].append(ckv[:tp].reshape(nbp, seq, -1))
                out["kpe_p"].append(kpe[:tp].reshape(nbp, seq, -1))
                if ts:
                    ckv_s, kpe_s = ckv[tp:].reshape(ndb, dec_seq, -1), kpe[tp:].reshape(ndb, dec_seq, -1)
                    c_all = jnp.concatenate([cache_ckv[j], ckv_s], 1).reshape(-1, MLA_KV_LORA)
                    p_all = jnp.concatenate([cache_kpe[j], kpe_s], 1).reshape(-1, MLA_ROPE)
                    n_all = c_all.shape[0]
                    k_s, v_s = _kv_expand(c_all, p_all, exp_w, 512 if n_all % 512 == 0 else past + dec_seq, n_all,
                                          False)
                    o_s = _attention_sample(q, k_s, v_s, nb=ndb, tq=dec_seq, kv_len=past + dec_seq,
                                            q_blk0=tp // dec_seq)
                    out["ckv_s"].append(ckv_s)
                    out["kpe_s"].append(kpe_s)
            else:
                gw, wo, gn = gla_w[j]
                q, k, v, g, la = _gla_proj(x, gw, tm)
                zeros = jnp.zeros((nbp, GLA_HEADS, GLA_DV, GLA_DK), F32)
                o, st_p = _gla_chunks(q, k, v, g, la, zeros, gn, nb=nbp, nt=seq // tq, tb=tq, blk0=0)
                o_s = None
                out["gla_p"].append(jnp.swapaxes(st_p, -1, -2))
                if ts:
                    s0 = jnp.swapaxes(state_gla[j], -1, -2)
                    o_s, st_s = _gla_chunks(q, k, v, g, la, s0, gn, nb=ndb, nt=1, tb=dec_seq, blk0=tp // dec_seq)
                    out["gla_s"].append(jnp.swapaxes(st_s, -1, -2))
            y1, ya, yb = _proj_ln(o, o_s, x, wo, ln1[i][0], ln1[i][1], tm, transposed=(i % 2 == 0))
            x = _moe_layer(y1, ya, yb, moe_w[i], i, tm, tr, upper)
        out["y_p"] = x[:tp].reshape(nbp, seq, D_MODEL)
        if ts:
            out["y_s"] = x[tp:].reshape(ndb, dec_seq, D_MODEL)
        return out

    if nb % 2 == 0 and nb >= 2:
        h = nb // 2
        a, b = pipeline(x_prompt[:h], x_sample), pipeline(x_prompt[h:], None)
        cat = lambda key: jnp.concatenate([jnp.stack(a[key]), jnp.stack(b[key])], 1)
        y_p = jnp.concatenate([a["y_p"], b["y_p"]], 0)
        ckv_p, kpe_p, gla_p = cat("ckv_p"), cat("kpe_p"), cat("gla_p")
    else:
        a = pipeline(x_prompt, x_sample)
        y_p = a["y_p"]
        ckv_p, kpe_p, gla_p = jnp.stack(a["ckv_p"]), jnp.stack(a["kpe_p"]), jnp.stack(a["gla_p"])
    return (y_p, a["y_s"], ckv_p, kpe_p, gla_p,
            jnp.stack(a["ckv_s"]), jnp.stack(a["kpe_s"]), jnp.stack(a["gla_s"]))
```
